```python
import math
import jax, jax.numpy as jnp
from jax import lax
import numpy as np

D_MODEL = 1024
BATCH = 8
SEQ = 4096
DEPTH = 1

CHUNK = 64
N_MEM = 256
Q_BLOCK = 128
SSM_WIDTH = D_MODEL // 2
SSM_GROUP = 16
SSM_GROUPS = SSM_WIDTH // SSM_GROUP
SSM_STATE = 64
DT_MIN = 1e-3
DT_MAX = 1e-1
DIFF_HEADS = 4
DIFF_HEAD_DIM = D_MODEL // 16
DIFF_QK = 2 * DIFF_HEADS * DIFF_HEAD_DIM
DIFF_V = DIFF_HEADS * 2 * DIFF_HEAD_DIM
MEM_HEADS = 4
MEM_HEAD_DIM = D_MODEL // 8
MEM_WIDTH = MEM_HEADS * MEM_HEAD_DIM
N_BRANCH = 3
REL_BUCKETS = 32
REL_MAX_DIST = 128
LN_EPS = 1e-5
RMS_EPS = 1e-5
NEG_INF = -1e30
DEEPNORM_ALPHA = (2.0 * DEPTH) ** 0.25
DEEPNORM_BETA = (8.0 * DEPTH) ** -0.25
SPLITS = [SSM_WIDTH, SSM_WIDTH, DIFF_QK, DIFF_QK, DIFF_V, DIFF_V, MEM_WIDTH, MEM_WIDTH, N_BRANCH * D_MODEL]
D_IN = SSM_WIDTH * 2 + DIFF_QK * 2 + DIFF_V * 2 + MEM_WIDTH * 2 + N_BRANCH * D_MODEL

kernel_name = "hybrid_s5_diffattn_memxattn_deepnorm"


def _split_points():
    pts, acc = [], 0
    for s in SPLITS[:-1]:
        acc += s
        pts.append(acc)
    return pts


def _layer_norm(h, g, b):
    hf = h.astype(jnp.float32)
    mu = jnp.mean(hf, axis=-1, keepdims=True)
    var = jnp.mean(jnp.square(hf - mu), axis=-1, keepdims=True)
    return ((hf - mu) * lax.rsqrt(var + LN_EPS) * g + b).astype(h.dtype)


def _t5_bucket(rel):
    half = REL_BUCKETS // 2
    max_exact = half // 2
    ret = jnp.where(rel > 0, half, 0)
    n = jnp.abs(rel)
    large = max_exact + (jnp.log(jnp.maximum(n, 1).astype(jnp.float32) / max_exact)
                         / math.log(REL_MAX_DIST / max_exact) * (half - max_exact)).astype(jnp.int32)
    large = jnp.minimum(large, half - 1)
    return ret + jnp.where(n < max_exact, n, large)


def _scan_op(e1, e2):
    a1, b1 = e1
    a2, b2 = e2
    return a2 * a1, a2 * b1 + b2


def _s5_branch(u, lam_re, lam_im, log_dt, b_re, b_im, c_re, c_im, d_skip, w_glu):
    bsz, seq, _ = u.shape
    uf = u.astype(jnp.float32)
    lam = lax.complex(lam_re.astype(jnp.float32), lam_im.astype(jnp.float32))
    dt = jnp.exp(log_dt.astype(jnp.float32))[:, None]
    lam_bar = jnp.exp(lam * dt)
    b = lax.complex(b_re.astype(jnp.float32), b_im.astype(jnp.float32))
    b_bar = ((lam_bar - 1.0) / lam)[..., None] * b
    ug = uf.reshape(bsz, seq, SSM_GROUPS, SSM_GROUP).astype(jnp.complex64)
    bu = jnp.einsum('bsgh,gph->bsgp', ug, b_bar)
    a = jnp.broadcast_to(lam_bar, (1, seq) + lam_bar.shape)
    _, states = lax.associative_scan(_scan_op, (a, bu), axis=1)
    c = lax.complex(c_re.astype(jnp.float32), c_im.astype(jnp.float32))
    y = jnp.real(jnp.einsum('bsgp,ghp->bsgh', states, c)).reshape(bsz, seq, SSM_WIDTH)
    y = jax.nn.gelu(y + d_skip.astype(jnp.float32) * uf)
    val, gate = jnp.split(y @ w_glu.astype(jnp.float32), 2, axis=-1)
    return (val * jax.nn.sigmoid(gate)).astype(u.dtype)


def _diff_attention(q, k, v, lambda_q1, lambda_k1, lambda_q2, lambda_k2, subln_w, rel_bias, lambda_init):
    bsz, seq, _ = q.shape
    scale = DIFF_HEAD_DIM ** -0.5
    q = q.reshape(bsz, seq, DIFF_HEADS, 2, DIFF_HEAD_DIM) * scale
    k = k.reshape(bsz, seq, DIFF_HEADS, 2, DIFF_HEAD_DIM)
    v = v.reshape(bsz, seq, DIFF_HEADS, 2 * DIFF_HEAD_DIM)
    lam = (jnp.exp(jnp.sum(lambda_q1.astype(jnp.float32) * lambda_k1.astype(jnp.float32)))
           - jnp.exp(jnp.sum(lambda_q2.astype(jnp.float32) * lambda_k2.astype(jnp.float32)))
           + lambda_init)
    dist = jnp.arange(-(seq - 1), seq)
    bias_d = rel_bias[_t5_bucket(dist)].astype(jnp.float32)
    n_blk = seq // Q_BLOCK
    qb = q.reshape(bsz, n_blk, Q_BLOCK, DIFF_HEADS, 2, DIFF_HEAD_DIM).transpose(1, 0, 2, 3, 4, 5)
    kpos = jnp.arange(seq)

    def attend(args):
        q_blk, blk = args
        qpos = blk * Q_BLOCK + jnp.arange(Q_BLOCK)
        s = jnp.einsum('bqhcd,bkhcd->bhcqk', q_blk, k).astype(jnp.float32)
        bias = bias_d[kpos[None, :] - qpos[:, None] + seq - 1]
        s = s + bias.transpose(2, 0, 1)[None, :, None]
        allowed = (kpos[None, :] // CHUNK) <= (qpos[:, None] // CHUNK)
        p = jax.nn.softmax(jnp.where(allowed, s, NEG_INF), axis=-1)
        a = p[:, :, 0] - lam * p[:, :, 1]
        return jnp.einsum('bhqk,bkhe->bqhe', a.astype(v.dtype), v)

    o = lax.map(attend, (qb, jnp.arange(n_blk)))
    o = o.transpose(1, 0, 2, 3, 4).reshape(bsz, seq, DIFF_HEADS, 2 * DIFF_HEAD_DIM).astype(jnp.float32)
    o = o * lax.rsqrt(jnp.mean(jnp.square(o), axis=-1, keepdims=True) + RMS_EPS) * subln_w
    o = o * (1.0 - lambda_init)
    return o.reshape(bsz, seq, DIFF_V)


def _memory_attention(q, mk, mv):
    bsz, seq, _ = q.shape
    n_mem = mk.shape[1]
    q = q.reshape(bsz, seq, MEM_HEADS, MEM_HEAD_DIM) * (MEM_HEAD_DIM ** -0.5)
    mk = mk.reshape(bsz, n_mem, MEM_HEADS, MEM_HEAD_DIM)
    mv = mv.reshape(bsz, n_mem, MEM_HEADS, MEM_HEAD_DIM)
    s = jnp.einsum('bshd,bmhd->bhsm', q, mk).astype(jnp.float32)
    p = jax.nn.softmax(s, axis=-1)
    o = jnp.einsum('bhsm,bmhd->bshd', p.astype(mv.dtype), mv)
    return o.reshape(bsz, seq, MEM_WIDTH)


def setup_inputs(seed: int = 0) -> dict:
    key = jax.random.key(seed)
    ks = jax.random.split(key, 32)
    nrm = jax.random.normal
    f32 = jnp.float32
    pts = _split_points()
    x = nrm(ks[0], (BATCH, SEQ, D_MODEL), f32)
    mem = nrm(ks[1], (BATCH, N_MEM, D_MODEL), f32)
    w_in = nrm(ks[2], (DEPTH, D_MODEL, D_IN), f32) * D_MODEL ** -0.5
    w_in = w_in.at[:, :, pts[3]:pts[4]].multiply(DEEPNORM_BETA)
    lam_re = -0.5 + 0.01 * nrm(ks[3], (DEPTH, SSM_GROUPS, SSM_STATE), f32)
    lam_im = (jnp.pi * jnp.arange(SSM_STATE, dtype=f32))[None, None, :] + 0.01 * nrm(ks[4], (DEPTH, SSM_GROUPS, SSM_STATE), f32)
    log_dt = jax.random.uniform(ks[5], (DEPTH, SSM_GROUPS), f32, math.log(DT_MIN), math.log(DT_MAX))
    b_re = nrm(ks[6], (DEPTH, SSM_GROUPS, SSM_STATE, SSM_GROUP), f32) * (2.0 * SSM_GROUP) ** -0.5
    b_im = nrm(ks[7], (DEPTH, SSM_GROUPS, SSM_STATE, SSM_GROUP), f32) * (2.0 * SSM_GROUP) ** -0.5
    c_re = nrm(ks[8], (DEPTH, SSM_GROUPS, SSM_GROUP, SSM_STATE), f32) * SSM_STATE ** -0.5
    c_im = nrm(ks[9], (DEPTH, SSM_GROUPS, SSM_GROUP, SSM_STATE), f32) * SSM_STATE ** -0.5
    d_skip = nrm(ks[10], (DEPTH, SSM_WIDTH), f32)
    w_glu = nrm(ks[11], (DEPTH, SSM_WIDTH, 2 * SSM_WIDTH), f32) * SSM_WIDTH ** -0.5
    lambda_q1 = 0.1 * nrm(ks[12], (DEPTH, DIFF_HEAD_DIM), f32)
    lambda_k1 = 0.1 * nrm(ks[13], (DEPTH, DIFF_HEAD_DIM), f32)
    lambda_q2 = 0.1 * nrm(ks[14], (DEPTH, DIFF_HEAD_DIM), f32)
    lambda_k2 = 0.1 * nrm(ks[15], (DEPTH, DIFF_HEAD_DIM), f32)
    subln_w = 1.0 + 0.01 * nrm(ks[16], (DEPTH, 2 * DIFF_HEAD_DIM), f32)
    rel_bias = 0.5 * nrm(ks[17], (REL_BUCKETS, DIFF_HEADS), f32)
    w_mem_kv = nrm(ks[18], (DEPTH, D_MODEL, 2 * MEM_WIDTH), f32) * D_MODEL ** -0.5
    w_mem_kv = w_mem_kv.at[:, :, MEM_WIDTH:].multiply(DEEPNORM_BETA)
    w_br_ssm = nrm(ks[19], (DEPTH, SSM_WIDTH, D_MODEL), f32) * SSM_WIDTH ** -0.5 * DEEPNORM_BETA
    w_br_diff = nrm(ks[20], (DEPTH, DIFF_V, D_MODEL), f32) * DIFF_V ** -0.5 * DEEPNORM_BETA
    w_br_mem = nrm(ks[21], (DEPTH, MEM_WIDTH, D_MODEL), f32) * MEM_WIDTH ** -0.5 * DEEPNORM_BETA
    w_out = nrm(ks[22], (DEPTH, D_MODEL, D_MODEL), f32) * D_MODEL ** -0.5 * DEEPNORM_BETA
    ln_g = 1.0 + 0.01 * nrm(ks[23], (DEPTH, D_MODEL), f32)
    ln_b = 0.01 * nrm(ks[24], (DEPTH, D_MODEL), f32)
    return {"x": x, "mem": mem, "w_in": w_in, "lam_re": lam_re, "lam_im": lam_im, "log_dt": log_dt,
            "b_re": b_re, "b_im": b_im, "c_re": c_re, "c_im": c_im, "d_skip": d_skip, "w_glu": w_glu,
            "lambda_q1": lambda_q1, "lambda_k1": lambda_k1, "lambda_q2": lambda_q2, "lambda_k2": lambda_k2,
            "subln_w": subln_w, "rel_bias": rel_bias, "w_mem_kv": w_mem_kv, "w_br_ssm": w_br_ssm,
            "w_br_diff": w_br_diff, "w_br_mem": w_br_mem, "w_out": w_out, "ln_g": ln_g, "ln_b": ln_b}


def reference(x, mem, w_in, lam_re, lam_im, log_dt, b_re, b_im, c_re, c_im, d_skip, w_glu,
              lambda_q1, lambda_k1, lambda_q2, lambda_k2, subln_w, rel_bias, w_mem_kv,
              w_br_ssm, w_br_diff, w_br_mem, w_out, ln_g, ln_b):
    bsz, seq, _ = x.shape
    pts = _split_points()
    h = x
    for layer in range(DEPTH):
        lambda_init = 0.8 - 0.6 * math.exp(-0.3 * layer)
        proj = h @ w_in[layer]
        u, z_ssm, dq, dk, dv, z_diff, mq, z_mem, gates = jnp.split(proj, pts, axis=-1)
        y_ssm = _s5_branch(u, lam_re[layer], lam_im[layer], log_dt[layer], b_re[layer], b_im[layer],
                           c_re[layer], c_im[layer], d_skip[layer], w_glu[layer])
        y_ssm = (y_ssm * jax.nn.silu(z_ssm)).astype(h.dtype)
        y_diff = _diff_attention(dq, dk, dv, lambda_q1[layer], lambda_k1[layer], lambda_q2[layer],
                                 lambda_k2[layer], subln_w[layer], rel_bias, lambda_init)
        y_diff = (y_diff * jax.nn.silu(z_diff.astype(jnp.float32))).astype(h.dtype)
        mk, mv = jnp.split(mem @ w_mem_kv[layer], 2, axis=-1)
        y_mem = (_memory_attention(mq, mk, mv) * jax.nn.silu(z_mem)).astype(h.dtype)
        g = jax.nn.sigmoid(gates.astype(jnp.float32)).reshape(bsz, seq, N_BRANCH, D_MODEL)
        merged = (g[:, :, 0] * (y_ssm @ w_br_ssm[layer])
                  + g[:, :, 1] * (y_diff @ w_br_diff[layer])
                  + g[:, :, 2] * (y_mem @ w_br_mem[layer]))
        out = merged.astype(h.dtype) @ w_out[layer]
        h = _layer_norm(DEEPNORM_ALPHA * h + out, ln_g[layer], ln_b[layer])
    return h
```

```python
import functools
import math

import jax
import jax.numpy as jnp
from jax import lax
from jax.experimental import pallas as pl
from jax.experimental.pallas import tpu as pltpu

F32 = jnp.float32
BF16 = jnp.bfloat16

LANES = 128
VMEM_LIMIT_BYTES = 56 * 1024 * 1024

CHUNK = 64
SSM_GROUP = 16
SSM_STATE = 64
SSM_L = 16
DIFF_HEADS = 4
DIFF_HEAD_DIM = 64
MEM_HEADS = 4
MEM_HEAD_DIM = 128
REL_BUCKETS = 32
REL_MAX_DIST = 128
LN_EPS = 1e-5
RMS_EPS = 1e-5
NEG_INF = -1e30

PROJ_TM = 512
DIFF_TQ = 512
MERGE_TM = 512


def _cparams(sem):
    return pltpu.CompilerParams(dimension_semantics=sem, vmem_limit_bytes=VMEM_LIMIT_BYTES)


def _silu(z):
    return z * jax.nn.sigmoid(z)


def _proj_kernel(x_ref, w_ref, u_ref, szs_ref, q_ref, k_ref, v_ref, szd_ref, mq_ref, szm_ref, g_ref,
                 *, width, mem_scale, diff_scale):
    xb = x_ref[...].astype(BF16)

    def mm(col, ncols):
        return jnp.dot(xb, w_ref[:, col:col + ncols], preferred_element_type=F32)

    w = width
    u_ref[...] = mm(0, w)
    szs_ref[...] = _silu(mm(w, w)).astype(BF16)
    q_ref[...] = (mm(2 * w, w) * diff_scale).astype(BF16)
    k_ref[...] = mm(3 * w, w).astype(BF16)
    v_ref[...] = mm(4 * w, w).astype(BF16)
    szd_ref[...] = _silu(mm(5 * w, w)).astype(BF16)
    mq_ref[...] = (mm(6 * w, w) * mem_scale).astype(BF16)
    szm_ref[...] = _silu(mm(7 * w, w)).astype(BF16)
    for j in range(6):
        g_ref[:, j * w:(j + 1) * w] = jax.nn.sigmoid(mm(8 * w + j * w, w)).astype(BF16)


def _projection(x2, w_bf, width):
    t, d = x2.shape
    d_in = w_bf.shape[1]
    tm = PROJ_TM
    row = lambda n: pl.BlockSpec((tm, n), lambda i: (i, 0))
    outs = [jax.ShapeDtypeStruct((t, width), F32)] + [jax.ShapeDtypeStruct((t, width), BF16)] * 7 \
        + [jax.ShapeDtypeStruct((t, 6 * width), BF16)]
    return pl.pallas_call(
        functools.partial(_proj_kernel, width=width, mem_scale=MEM_HEAD_DIM ** -0.5,
                          diff_scale=DIFF_HEAD_DIM ** -0.5),
        grid=(t // tm,),
        in_specs=[row(d), pl.BlockSpec((d, d_in), lambda i: (0, 0), pipeline_mode=pl.Buffered(1))],
        out_specs=[row(width)] * 8 + [row(6 * width)],
        out_shape=outs,
        compiler_params=_cparams(("parallel",)),
        name="in_proj",
    )(x2, w_bf)


def _memkv_kernel(m_ref, w_ref, o_ref):
    o_ref[...] = jnp.dot(m_ref[...].astype(BF16), w_ref[...], preferred_element_type=F32).astype(BF16)


def _memkv(mem2, w_bf):
    r, d = mem2.shape
    n = w_bf.shape[1]
    tm = 512
    return pl.pallas_call(
        _memkv_kernel,
        grid=(r // tm,),
        in_specs=[pl.BlockSpec((tm, d), lambda i: (i, 0)), pl.BlockSpec((d, n), lambda i: (0, 0))],
        out_specs=pl.BlockSpec((tm, n), lambda i: (i, 0)),
        out_shape=jax.ShapeDtypeStruct((r, n), BF16),
        compiler_params=_cparams(("parallel",)),
        name="mem_kv",
    )(mem2, w_bf)


def _s5_matrices(lam_re, lam_im, log_dt, b_re, b_im, c_re, c_im):
    hp = lax.Precision.HIGHEST
    L = SSM_L
    g_n, p_n = lam_re.shape
    h_n = b_re.shape[-1]
    gl = LANES // h_n
    nb = g_n // gl
    dt = jnp.exp(log_dt.astype(F32))[:, None]
    zr = lam_re.astype(F32) * dt
    zi = lam_im.astype(F32) * dt
    ks = jnp.arange(L + 1, dtype=F32)[:, None, None]
    mag = jnp.exp(zr[None] * ks)
    pr = mag * jnp.cos(zi[None] * ks)
    pi = mag * jnp.sin(zi[None] * ks)
    nr, ni = pr[1] - 1.0, pi[1]
    den = lam_re * lam_re + lam_im * lam_im
    fr = (nr * lam_re + ni * lam_im) / den
    fi = (ni * lam_re - nr * lam_im) / den
    bbr = fr[..., None] * b_re - fi[..., None] * b_im
    bbi = fr[..., None] * b_im + fi[..., None] * b_re
    cpr = c_re[None] * pr[:, :, None, :] - c_im[None] * pi[:, :, None, :]
    cpi = c_re[None] * pi[:, :, None, :] + c_im[None] * pr[:, :, None, :]
    kt = (jnp.einsum('kgop,gpi->kgio', cpr[:L], bbr, precision=hp)
          - jnp.einsum('kgop,gpi->kgio', cpi[:L], bbi, precision=hp))
    eye = jnp.eye(gl, dtype=F32)
    ii = jnp.arange(L)
    lag = ii[None, :] - ii[:, None]
    kexp = jnp.where((lag >= 0)[:, :, None, None, None], kt[jnp.clip(lag, 0, L - 1)], 0.0)
    kexp = kexp.reshape(L, L, nb, gl, h_n, h_n)
    kmat = jnp.einsum('jimgab,gh->mjgaihb', kexp, eye).reshape(nb, L * LANES, L * LANES)
    prj, pij = pr[L - 1 - ii], pi[L - 1 - ii]
    wr = prj[..., None] * bbr[None] - pij[..., None] * bbi[None]
    wi = prj[..., None] * bbi[None] + pij[..., None] * bbr[None]
    wri = jnp.stack([wr, wi], axis=0).reshape(2, L, nb, gl, p_n, h_n)
    bmat = jnp.einsum('rjmgpa,gh->mjgarhp', wri, eye).reshape(nb, L * LANES, 2 * gl * p_n)
    cri = jnp.stack([cpr[1:], -cpi[1:]], axis=0).reshape(2, L, nb, gl, h_n, p_n)
    cmat = jnp.einsum('rimgbp,gh->mrgpihb', cri, eye).reshape(nb, 2 * gl * p_n, L * LANES)
    a_re = pr[L].reshape(nb, 1, gl * p_n)
    a_im = pi[L].reshape(nb, 1, gl * p_n)
    return kmat.astype(BF16), bmat.astype(BF16), cmat.astype(BF16), a_re, a_im


def _s5_kernel(u_ref, k_ref, b_ref, c_ref, ar_ref, ai_ref, y_ref, u2_ref, v_ref, sp_ref, y2_ref,
               *, n_chunks, n_state):
    L = SSM_L
    for i in range(L):
        u2_ref[:, i * LANES:(i + 1) * LANES] = u_ref[pl.ds(i, n_chunks, stride=L), :].astype(BF16)
    v_ref[...] = jnp.dot(u2_ref[...], b_ref[...], preferred_element_type=F32)
    ar = ar_ref[...]
    ai = ai_ref[...]

    def step(c, carry):
        sr, si = carry
        sp_ref[pl.ds(c, 1), 0:n_state] = sr
        sp_ref[pl.ds(c, 1), n_state:2 * n_state] = si
        vr = v_ref[pl.ds(c, 1), 0:n_state]
        vi = v_ref[pl.ds(c, 1), n_state:2 * n_state]
        return ar * sr - ai * si + vr, ar * si + ai * sr + vi

    zero = jnp.zeros((1, n_state), F32)
    lax.fori_loop(0, n_chunks, step, (zero, zero))
    y2_ref[...] = (jnp.dot(u2_ref[...], k_ref[...], preferred_element_type=F32)
                   + jnp.dot(sp_ref[...].astype(BF16), c_ref[...], preferred_element_type=F32))
    for i in range(L):
        y_ref[pl.ds(i, n_chunks, stride=L), :] = y2_ref[:, i * LANES:(i + 1) * LANES]


def _s5_core(u3, kmat, bmat, cmat, a_re, a_im):
    bsz, seq, width = u3.shape
    nb = width // LANES
    n_chunks = seq // SSM_L
    n_state = a_re.shape[-1]
    kdim = SSM_L * LANES
    const = lambda shape: pl.BlockSpec((None,) + shape, lambda m, b: (m, 0, 0), pipeline_mode=pl.Buffered(1))
    return pl.pallas_call(
        functools.partial(_s5_kernel, n_chunks=n_chunks, n_state=n_state),
        grid=(nb, bsz),
        in_specs=[pl.BlockSpec((None, seq, LANES), lambda m, b: (b, 0, m)),
                  const((kdim, kdim)), const((kdim, 2 * n_state)), const((2 * n_state, kdim)),
                  const((1, n_state)), const((1, n_state))],
        out_specs=pl.BlockSpec((None, seq, LANES), lambda m, b: (b, 0, m)),
        out_shape=jax.ShapeDtypeStruct((bsz, seq, width), F32),
        scratch_shapes=[pltpu.VMEM((n_chunks, kdim), BF16), pltpu.VMEM((n_chunks, 2 * n_state), F32),
                        pltpu.VMEM((n_chunks, 2 * n_state), F32), pltpu.VMEM((n_chunks, kdim), F32)],
        compiler_params=_cparams(("parallel", "parallel")),
        name="s5_core",
    )(u3, kmat, bmat, cmat, a_re, a_im)


def _t5_bucket(rel):
    half = REL_BUCKETS // 2
    max_exact = half // 2
    ret = jnp.where(rel > 0, half, 0)
    n = jnp.abs(rel)
    large = max_exact + (jnp.log(jnp.maximum(n, 1).astype(jnp.float32) / max_exact)
                         / math.log(REL_MAX_DIST / max_exact) * (half - max_exact)).astype(jnp.int32)
    large = jnp.minimum(large, half - 1)
    return ret + jnp.where(n < max_exact, n, large)


def _bias_tiles(rel_bias, seq, tq):
    assert tq >= REL_MAX_DIST and tq % CHUNK == 0
    dist = jnp.arange(-(seq - 1), seq)
    table = rel_bias[_t5_bucket(dist)].astype(F32)
    table = table - table[0][None, :]
    r = jnp.arange(tq)[:, None]
    c = jnp.arange(tq)[None, :]
    diag = table[(c - r) + seq - 1]
    diag = jnp.where(((c // CHUNK) <= (r // CHUNK))[:, :, None], diag, NEG_INF)
    adj = table[(c - r - tq) + seq - 1]
    return diag.transpose(2, 0, 1), adj.transpose(2, 0, 1)


def _diff_kernel(lam_ref, q_ref, k_ref, v_ref, z_ref, bd_ref, ba_ref, w_ref, o_ref,
                 m_ref, l_ref, acc_ref, *, tq, out_scale):
    qi = pl.program_id(1)
    lam = lam_ref[0]
    lane = lax.broadcasted_iota(jnp.int32, (tq, LANES), 1)
    nt = (((1,), (1,)), ((), ()))
    hd = 2 * DIFF_HEAD_DIM

    for h in range(DIFF_HEADS):
        cols = pl.ds(h * hd, hd)
        qh = q_ref[:, cols].astype(F32)
        qs = (jnp.where(lane < DIFF_HEAD_DIM, qh, 0.0).astype(BF16),
              jnp.where(lane >= DIFF_HEAD_DIM, qh, 0.0).astype(BF16))
        m_ref[...] = jnp.full(m_ref.shape, NEG_INF, F32)
        l_ref[...] = jnp.zeros(l_ref.shape, F32)
        acc_ref[...] = jnp.zeros(acc_ref.shape, F32)

        def tile(k0, bias):
            kt = k_ref[pl.ds(k0, tq), cols]
            vt = v_ref[pl.ds(k0, tq), cols]
            for c in range(2):
                s = lax.dot_general(qs[c], kt, nt, preferred_element_type=F32)
                if bias is not None:
                    s = s + bias
                m_old = m_ref[c]
                m_new = jnp.maximum(m_old, jnp.max(s, axis=-1, keepdims=True))
                alpha = jnp.exp(m_old - m_new)
                p = jnp.exp(s - m_new)
                l_ref[c] = alpha * l_ref[c] + jnp.sum(p, axis=-1, keepdims=True)
                acc_ref[c] = alpha * acc_ref[c] + jnp.dot(p.astype(BF16), vt, preferred_element_type=F32)
                m_ref[c] = m_new

        def far(ki, carry):
            tile(pl.multiple_of(ki * tq, tq), None)
            return carry

        lax.fori_loop(0, jnp.maximum(qi - 1, 0), far, 0)

        @pl.when(qi >= 1)
        def _():
            tile(pl.multiple_of((qi - 1) * tq, tq), ba_ref[h])

        tile(pl.multiple_of(qi * tq, tq), bd_ref[h])

        o = acc_ref[0] / l_ref[0] - lam * (acc_ref[1] / l_ref[1])
        o = o * lax.rsqrt(jnp.mean(o * o, axis=-1, keepdims=True) + RMS_EPS) * w_ref[...]
        o = o * out_scale
        o_ref[:, cols] = (o * z_ref[:, cols].astype(F32)).astype(o_ref.dtype)


def _diff_attention(lam, q3, k3, v3, sz3, bias_diag, bias_adj, subln_w, lambda_init):
    bsz, seq, width = q3.shape
    tq = DIFF_TQ
    hd = 2 * DIFF_HEAD_DIM
    qspec = pl.BlockSpec((None, tq, width), lambda b, i: (b, i, 0))
    kvspec = pl.BlockSpec((None, seq, width), lambda b, i: (b, 0, 0))
    bspec = pl.BlockSpec((DIFF_HEADS, tq, tq), lambda b, i: (0, 0, 0), pipeline_mode=pl.Buffered(1))
    return pl.pallas_call(
        functools.partial(_diff_kernel, tq=tq, out_scale=1.0 - lambda_init),
        grid=(bsz, seq // tq),
        in_specs=[pl.BlockSpec(memory_space=pltpu.SMEM), qspec, kvspec, kvspec, qspec, bspec, bspec,
                  pl.BlockSpec((1, hd), lambda b, i: (0, 0))],
        out_specs=qspec,
        out_shape=jax.ShapeDtypeStruct((bsz, seq, width), BF16),
        scratch_shapes=[pltpu.VMEM((2, tq, 1), F32), pltpu.VMEM((2, tq, 1), F32),
                        pltpu.VMEM((2, tq, hd), F32)],
        compiler_params=_cparams(("parallel", "arbitrary")),
        name="diff_attn",
    )(lam, q3, k3, v3, sz3, bias_diag, bias_adj, subln_w)


def _merge_kernel(x_ref, y_ref, u_ref, szs_ref, yd_ref, mq_ref, szm_ref, g_ref, mkv_ref,
                  dsk_ref, wglu_ref, wbs_ref, wbd_ref, wbm_ref, wout_ref, lng_ref, lnb_ref, o_ref,
                  *, width, d_model, alpha):
    nt = (((1,), (1,)), ((), ()))
    ya = jax.nn.gelu(y_ref[...] + dsk_ref[...] * u_ref[...])
    glu = jnp.dot(ya.astype(BF16), wglu_ref[...], preferred_element_type=F32)
    y_ssm = glu[:, :width] * jax.nn.sigmoid(glu[:, width:]) * szs_ref[...].astype(F32)
    merged = g_ref[:, 0:d_model].astype(F32) * jnp.dot(y_ssm.astype(BF16), wbs_ref[...],
                                                      preferred_element_type=F32)
    merged += g_ref[:, d_model:2 * d_model].astype(F32) * jnp.dot(yd_ref[...], wbd_ref[...],
                                                                  preferred_element_type=F32)
    heads = []
    for h in range(MEM_HEADS):
        cols = pl.ds(h * MEM_HEAD_DIM, MEM_HEAD_DIM)
        s = lax.dot_general(mq_ref[:, cols], mkv_ref[:, cols], nt, preferred_element_type=F32)
        p = jnp.exp(s - jnp.max(s, axis=-1, keepdims=True))
        l = jnp.sum(p, axis=-1, keepdims=True)
        o = jnp.dot(p.astype(BF16), mkv_ref[:, pl.ds(width + h * MEM_HEAD_DIM, MEM_HEAD_DIM)],
                    preferred_element_type=F32)
        heads.append(o / l)
    y_mem = jnp.concatenate(heads, axis=-1) * szm_ref[...].astype(F32)
    merged += g_ref[:, 2 * d_model:3 * d_model].astype(F32) * jnp.dot(y_mem.astype(BF16), wbm_ref[...],
                                                                      preferred_element_type=F32)
    out = jnp.dot(merged.astype(BF16), wout_ref[...], preferred_element_type=F32)
    hres = alpha * x_ref[...] + out
    mu = jnp.mean(hres, axis=-1, keepdims=True)
    cen = hres - mu
    var = jnp.mean(cen * cen, axis=-1, keepdims=True)
    o_ref[...] = cen * lax.rsqrt(var + LN_EPS) * lng_ref[...] + lnb_ref[...]


def _merge(x2, y2, u2, szs, yd, mq, szm, g, mkv, d_skip, w_glu, w_bs, w_bd, w_bm, w_out, ln_g, ln_b,
           seq, alpha):
    t, d_model = x2.shape
    width = y2.shape[1]
    n_mem = mkv.shape[1]
    tm = MERGE_TM
    per_b = seq // tm
    row = lambda n: pl.BlockSpec((tm, n), lambda i: (i, 0))
    full = lambda a: pl.BlockSpec(a.shape, lambda i: (0,) * a.ndim)
    return pl.pallas_call(
        functools.partial(_merge_kernel, width=width, d_model=d_model, alpha=alpha),
        grid=(t // tm,),
        in_specs=[row(d_model), row(width), row(width), row(width), row(width), row(width), row(width),
                  row(3 * d_model),
                  pl.BlockSpec((None, n_mem, 2 * width), lambda i: (i // per_b, 0, 0)),
                  full(d_skip), full(w_glu), full(w_bs), full(w_bd), full(w_bm), full(w_out),
                  full(ln_g), full(ln_b)],
        out_specs=row(d_model),
        out_shape=jax.ShapeDtypeStruct((t, d_model), F32),
        compiler_params=_cparams(("parallel",)),
        name="merge_out",
    )(x2, y2, u2, szs, yd, mq, szm, g, mkv, d_skip, w_glu, w_bs, w_bd, w_bm, w_out, ln_g, ln_b)


def kernel(x, mem, w_in, lam_re, lam_im, log_dt, b_re, b_im, c_re, c_im, d_skip, w_glu, lambda_q1, lambda_k1, lambda_q2, lambda_k2, subln_w, rel_bias, w_mem_kv, w_br_ssm, w_br_diff, w_br_mem, w_out, ln_g, ln_b):
    bsz, seq, d_model = x.shape
    depth = w_in.shape[0]
    width = d_model // 2
    n_mem = mem.shape[1]
    alpha = (2.0 * depth) ** 0.25
    bias_diag, bias_adj = _bias_tiles(rel_bias, seq, DIFF_TQ)
    mem2 = mem.reshape(bsz * n_mem, d_model)
    h = x.reshape(bsz * seq, d_model)
    for layer in range(depth):
        lambda_init = 0.8 - 0.6 * math.exp(-0.3 * layer)
        u, szs, dq, dk, dv, szd, mq, szm, g = _projection(h, w_in[layer].astype(BF16), width)
        mkv = _memkv(mem2, w_mem_kv[layer].astype(BF16)).reshape(bsz, n_mem, 2 * width)
        mats = _s5_matrices(lam_re[layer], lam_im[layer], log_dt[layer], b_re[layer], b_im[layer],
                            c_re[layer], c_im[layer])
        y = _s5_core(u.reshape(bsz, seq, width), *mats).reshape(bsz * seq, width)
        lam = (jnp.exp(jnp.sum(lambda_q1[layer].astype(F32) * lambda_k1[layer].astype(F32)))
               - jnp.exp(jnp.sum(lambda_q2[layer].astype(F32) * lambda_k2[layer].astype(F32)))
               + lambda_init).reshape(1)
        r3 = lambda a: a.reshape(bsz, seq, width)
        yd = _diff_attention(lam, r3(dq), r3(dk), r3(dv), r3(szd), bias_diag, bias_adj,
                             subln_w[layer].reshape(1, -1).astype(F32), lambda_init)
        h = _merge(h, y, u, szs, yd.reshape(bsz * seq, width), mq, szm, g, mkv,
                   d_skip[layer].reshape(1, -1), w_glu[layer].astype(BF16), w_br_ssm[layer].astype(BF16),
                   w_br_diff[layer].astype(BF16), w_br_mem[layer].astype(BF16), w_out[layer].astype(BF16),
                   ln_g[layer].reshape(1, -1), ln_b[layer].reshape(1, -1), seq, alpha)
    return h.reshape(bsz, seq, d_model)
```

```python
import functools
import math

import jax
import jax.numpy as jnp
from jax import lax
from jax.experimental import pallas as pl
from jax.experimental.pallas import tpu as pltpu

F32 = jnp.float32
BF16 = jnp.bfloat16

LANES = 128
VMEM_LIMIT_BYTES = 56 * 1024 * 1024

CHUNK = 64
SSM_GROUP = 16
SSM_STATE = 64
SSM_L = 16
DIFF_HEADS = 4
DIFF_HEAD_DIM = 64
MEM_HEADS = 4
MEM_HEAD_DIM = 128
REL_BUCKETS = 32
REL_MAX_DIST = 128
LN_EPS = 1e-5
RMS_EPS = 1e-5
NEG_INF = -1e30

PROJ_TM = 512
DIFF_TQ = 512
MERGE_TM = 512


def _cparams(sem):
    return pltpu.CompilerParams(dimension_semantics=sem, vmem_limit_bytes=VMEM_LIMIT_BYTES)


def _silu(z):
    return z * jax.nn.sigmoid(z)


def _proj_kernel(x_ref, w_ref, u_ref, szs_ref, q_ref, k_ref, v_ref, szd_ref, mq_ref, szm_ref, g_ref,
                 *, width, mem_scale, diff_scale):
    xb = x_ref[...].astype(BF16)

    def mm(col, ncols):
        return jnp.dot(xb, w_ref[:, col:col + ncols], preferred_element_type=F32)

    w = width
    u_ref[...] = mm(0, w)
    szs_ref[...] = _silu(mm(w, w)).astype(BF16)
    q_ref[...] = (mm(2 * w, w) * diff_scale).astype(BF16)
    k_ref[...] = mm(3 * w, w).astype(BF16)
    v_ref[...] = mm(4 * w, w).astype(BF16)
    szd_ref[...] = _silu(mm(5 * w, w)).astype(BF16)
    mq_ref[...] = (mm(6 * w, w) * mem_scale).astype(BF16)
    szm_ref[...] = _silu(mm(7 * w, w)).astype(BF16)
    for j in range(6):
        g_ref[:, j * w:(j + 1) * w] = jax.nn.sigmoid(mm(8 * w + j * w, w)).astype(BF16)


def _projection(x2, w_bf, width):
    t, d = x2.shape
    d_in = w_bf.shape[1]
    tm = PROJ_TM
    row = lambda n: pl.BlockSpec((tm, n), lambda i: (i, 0))
    outs = [jax.ShapeDtypeStruct((t, width), F32)] + [jax.ShapeDtypeStruct((t, width), BF16)] * 7 \
        + [jax.ShapeDtypeStruct((t, 6 * width), BF16)]
    return pl.pallas_call(
        functools.partial(_proj_kernel, width=width, mem_scale=MEM_HEAD_DIM ** -0.5,
                          diff_scale=DIFF_HEAD_DIM ** -0.5),
        grid=(t // tm,),
        in_specs=[row(d), pl.BlockSpec((d, d_in), lambda i: (0, 0), pipeline_mode=pl.Buffered(1))],
        out_specs=[row(width)] * 8 + [row(6 * width)],
        out_shape=outs,
        compiler_params=_cparams(("parallel",)),
        name="in_proj",
    )(x2, w_bf)


def _memkv_kernel(m_ref, w_ref, o_ref):
    o_ref[...] = jnp.dot(m_ref[...].astype(BF16), w_ref[...], preferred_element_type=F32).astype(BF16)


def _memkv(mem2, w_bf):
    r, d = mem2.shape
    n = w_bf.shape[1]
    tm = 512
    return pl.pallas_call(
        _memkv_kernel,
        grid=(r // tm,),
        in_specs=[pl.BlockSpec((tm, d), lambda i: (i, 0)), pl.BlockSpec((d, n), lambda i: (0, 0))],
        out_specs=pl.BlockSpec((tm, n), lambda i: (i, 0)),
        out_shape=jax.ShapeDtypeStruct((r, n), BF16),
        compiler_params=_cparams(("parallel",)),
        name="mem_kv",
    )(mem2, w_bf)


def _s5_matrices(lam_re, lam_im, log_dt, b_re, b_im, c_re, c_im):
    L = SSM_L
    g_n, p_n = lam_re.shape
    h_n = b_re.shape[-1]
    gl = LANES // h_n
    nb = g_n // gl
    sw = gl * p_n
    dt = jnp.exp(log_dt.astype(F32))[:, None]
    zr = lam_re.astype(F32) * dt
    zi = lam_im.astype(F32) * dt
    ks = jnp.arange(L + 1, dtype=F32)[:, None, None]
    mag = jnp.exp(zr[None] * ks)
    pr = mag * jnp.cos(zi[None] * ks)
    pi = mag * jnp.sin(zi[None] * ks)
    nr, ni = pr[1] - 1.0, pi[1]
    den = lam_re * lam_re + lam_im * lam_im
    fr = (nr * lam_re + ni * lam_im) / den
    fi = (ni * lam_re - nr * lam_im) / den
    bbr = fr[..., None] * b_re - fi[..., None] * b_im
    bbi = fr[..., None] * b_im + fi[..., None] * b_re
    gsel = jnp.arange(gl)[:, None]
    mask_s = ((jnp.arange(2 * sw)[None, :] % sw) // p_n == gsel).astype(F32)
    mask_o = ((jnp.arange(L * LANES)[None, :] % LANES) // h_n == gsel).astype(F32)

    rev = L - 1 - jnp.arange(L)
    pwl_r = pr[rev].reshape(L, g_n * p_n)
    pwl_i = pi[rev].reshape(L, g_n * p_n)
    bbl_r = bbr.transpose(2, 0, 1).reshape(h_n, g_n * p_n)
    bbl_i = bbi.transpose(2, 0, 1).reshape(h_n, g_n * p_n)
    w_r = pwl_r[:, None, :] * bbl_r[None] - pwl_i[:, None, :] * bbl_i[None]
    w_i = pwl_r[:, None, :] * bbl_i[None] + pwl_i[:, None, :] * bbl_r[None]
    blk = lambda w: w.reshape(L, h_n, nb, sw).transpose(2, 0, 1, 3)
    w_ri = jnp.concatenate([blk(w_r), blk(w_i)], axis=-1)
    bmat = (w_ri[:, :, None, :, :] * mask_s[None, None, :, None, :]).reshape(nb, L * LANES, 2 * sw)

    cl_r = c_re.transpose(2, 0, 1).reshape(p_n, g_n * h_n)
    cl_i = c_im.transpose(2, 0, 1).reshape(p_n, g_n * h_n)
    pwx_r = jnp.repeat(pr.transpose(0, 2, 1), h_n, axis=-1)
    pwx_i = jnp.repeat(pi.transpose(0, 2, 1), h_n, axis=-1)
    cp_r = cl_r[None] * pwx_r - cl_i[None] * pwx_i
    cp_i = cl_r[None] * pwx_i + cl_i[None] * pwx_r
    oblk = lambda c: c.reshape(L, p_n, nb, LANES).transpose(2, 1, 0, 3).reshape(nb, p_n, L * LANES)
    c_ri = jnp.stack([oblk(cp_r[1:]), oblk(-cp_i[1:])], axis=1)
    cmat = (c_ri[:, :, None, :, :] * mask_o[None, None, :, None, :]).reshape(nb, 2 * sw, L * LANES)

    bbx_r = jnp.repeat(bbr.transpose(2, 1, 0), h_n, axis=-1)
    bbx_i = jnp.repeat(bbi.transpose(2, 1, 0), h_n, axis=-1)
    kt = jnp.sum(cp_r[:L, None] * bbx_r[None] - cp_i[:L, None] * bbx_i[None], axis=2)
    base = kt.reshape(L, h_n, nb, LANES).transpose(2, 1, 0, 3).reshape(nb, h_n, L * LANES)
    rows = jnp.stack([jnp.pad(base, ((0, 0), (0, 0), (j * LANES, 0)))[:, :, :L * LANES] for j in range(L)],
                     axis=1)
    kmat = (rows[:, :, None, :, :] * mask_o[None, None, :, None, :]).reshape(nb, L * LANES, L * LANES)
    a_re = pr[L].reshape(nb, 1, sw)
    a_im = pi[L].reshape(nb, 1, sw)
    return kmat.astype(BF16), bmat.astype(BF16), cmat.astype(BF16), a_re, a_im


def _s5_kernel(u_ref, k_ref, b_ref, c_ref, ar_ref, ai_ref, y_ref, u2_ref, v_ref, sp_ref, y2_ref,
               *, n_chunks, n_state):
    L = SSM_L
    for i in range(L):
        u2_ref[:, i * LANES:(i + 1) * LANES] = u_ref[pl.ds(i, n_chunks, stride=L), :].astype(BF16)
    v_ref[...] = jnp.dot(u2_ref[...], b_ref[...], preferred_element_type=F32)
    ar = ar_ref[...]
    ai = ai_ref[...]

    def step(c, carry):
        sr, si = carry
        sp_ref[pl.ds(c, 1), 0:n_state] = sr
        sp_ref[pl.ds(c, 1), n_state:2 * n_state] = si
        vr = v_ref[pl.ds(c, 1), 0:n_state]
        vi = v_ref[pl.ds(c, 1), n_state:2 * n_state]
        return ar * sr - ai * si + vr, ar * si + ai * sr + vi

    zero = jnp.zeros((1, n_state), F32)
    lax.fori_loop(0, n_chunks, step, (zero, zero))
    y2_ref[...] = (jnp.dot(u2_ref[...], k_ref[...], preferred_element_type=F32)
                   + jnp.dot(sp_ref[...].astype(BF16), c_ref[...], preferred_element_type=F32))
    for i in range(L):
        y_ref[pl.ds(i, n_chunks, stride=L), :] = y2_ref[:, i * LANES:(i + 1) * LANES]


def _s5_core(u3, kmat, bmat, cmat, a_re, a_im):
    bsz, seq, width = u3.shape
    nb = width // LANES
    n_chunks = seq // SSM_L
    n_state = a_re.shape[-1]
    kdim = SSM_L * LANES
    const = lambda shape: pl.BlockSpec((None,) + shape, lambda m, b: (m, 0, 0), pipeline_mode=pl.Buffered(1))
    return pl.pallas_call(
        functools.partial(_s5_kernel, n_chunks=n_chunks, n_state=n_state),
        grid=(nb, bsz),
        in_specs=[pl.BlockSpec((None, seq, LANES), lambda m, b: (b, 0, m)),
                  const((kdim, kdim)), const((kdim, 2 * n_state)), const((2 * n_state, kdim)),
                  const((1, n_state)), const((1, n_state))],
        out_specs=pl.BlockSpec((None, seq, LANES), lambda m, b: (b, 0, m)),
        out_shape=jax.ShapeDtypeStruct((bsz, seq, width), F32),
        scratch_shapes=[pltpu.VMEM((n_chunks, kdim), BF16), pltpu.VMEM((n_chunks, 2 * n_state), F32),
                        pltpu.VMEM((n_chunks, 2 * n_state), F32), pltpu.VMEM((n_chunks, kdim), F32)],
        compiler_params=_cparams(("parallel", "parallel")),
        name="s5_core",
    )(u3, kmat, bmat, cmat, a_re, a_im)


def _t5_bucket(rel):
    half = REL_BUCKETS // 2
    max_exact = half // 2
    ret = jnp.where(rel > 0, half, 0)
    n = jnp.abs(rel)
    large = max_exact + (jnp.log(jnp.maximum(n, 1).astype(jnp.float32) / max_exact)
                         / math.log(REL_MAX_DIST / max_exact) * (half - max_exact)).astype(jnp.int32)
    large = jnp.minimum(large, half - 1)
    return ret + jnp.where(n < max_exact, n, large)


def _bias_tiles(rel_bias, seq, tq):
    assert tq >= REL_MAX_DIST and tq % CHUNK == 0

    def bias_of(rel):
        bucket = _t5_bucket(rel)
        out = jnp.zeros((rel_bias.shape[1],) + rel.shape, F32)
        for b in range(REL_BUCKETS):
            out = jnp.where((bucket == b)[None], rel_bias[b].astype(F32)[:, None, None], out)
        return out

    far = bias_of(jnp.full((1, 1), -(seq - 1), jnp.int32))
    r = jnp.arange(tq)[:, None]
    c = jnp.arange(tq)[None, :]
    diag = bias_of(c - r) - far
    diag = jnp.where(((c // CHUNK) <= (r // CHUNK))[None], diag, NEG_INF)
    adj = bias_of(c - r - tq) - far
    return diag, adj


def _diff_kernel(lam_ref, q_ref, k_ref, v_ref, z_ref, bd_ref, ba_ref, w_ref, o_ref,
                 m_ref, l_ref, acc_ref, *, tq, out_scale):
    qi = pl.program_id(1)
    lam = lam_ref[0]
    lane = lax.broadcasted_iota(jnp.int32, (tq, LANES), 1)
    nt = (((1,), (1,)), ((), ()))
    hd = 2 * DIFF_HEAD_DIM

    for h in range(DIFF_HEADS):
        cols = pl.ds(h * hd, hd)
        qh = q_ref[:, cols].astype(F32)
        qs = (jnp.where(lane < DIFF_HEAD_DIM, qh, 0.0).astype(BF16),
              jnp.where(lane >= DIFF_HEAD_DIM, qh, 0.0).astype(BF16))
        m_ref[...] = jnp.full(m_ref.shape, NEG_INF, F32)
        l_ref[...] = jnp.zeros(l_ref.shape, F32)
        acc_ref[...] = jnp.zeros(acc_ref.shape, F32)

        def tile(k0, bias):
            kt = k_ref[pl.ds(k0, tq), cols]
            vt = v_ref[pl.ds(k0, tq), cols]
            for c in range(2):
                s = lax.dot_general(qs[c], kt, nt, preferred_element_type=F32)
                if bias is not None:
                    s = s + bias
                m_old = m_ref[c]
                m_new = jnp.maximum(m_old, jnp.max(s, axis=-1, keepdims=True))
                alpha = jnp.exp(m_old - m_new)
                p = jnp.exp(s - m_new)
                l_ref[c] = alpha * l_ref[c] + jnp.sum(p, axis=-1, keepdims=True)
                acc_ref[c] = alpha * acc_ref[c] + jnp.dot(p.astype(BF16), vt, preferred_element_type=F32)
                m_ref[c] = m_new

        def far(ki, carry):
            tile(pl.multiple_of(ki * tq, tq), None)
            return carry

        lax.fori_loop(0, jnp.maximum(qi - 1, 0), far, 0)

        @pl.when(qi >= 1)
        def _():
            tile(pl.multiple_of((qi - 1) * tq, tq), ba_ref[h])

        tile(pl.multiple_of(qi * tq, tq), bd_ref[h])

        o = acc_ref[0] / l_ref[0] - lam * (acc_ref[1] / l_ref[1])
        o = o * lax.rsqrt(jnp.mean(o * o, axis=-1, keepdims=True) + RMS_EPS) * w_ref[...]
        o = o * out_scale
        o_ref[:, cols] = (o * z_ref[:, cols].astype(F32)).astype(o_ref.dtype)


def _diff_attention(lam, q3, k3, v3, sz3, bias_diag, bias_adj, subln_w, lambda_init):
    bsz, seq, width = q3.shape
    tq = DIFF_TQ
    hd = 2 * DIFF_HEAD_DIM
    qspec = pl.BlockSpec((None, tq, width), lambda b, i: (b, i, 0))
    kvspec = pl.BlockSpec((None, seq, width), lambda b, i: (b, 0, 0))
    bspec = pl.BlockSpec((DIFF_HEADS, tq, tq), lambda b, i: (0, 0, 0), pipeline_mode=pl.Buffered(1))
    return pl.pallas_call(
        functools.partial(_diff_kernel, tq=tq, out_scale=1.0 - lambda_init),
        grid=(bsz, seq // tq),
        in_specs=[pl.BlockSpec(memory_space=pltpu.SMEM), qspec, kvspec, kvspec, qspec, bspec, bspec,
                  pl.BlockSpec((1, hd), lambda b, i: (0, 0))],
        out_specs=qspec,
        out_shape=jax.ShapeDtypeStruct((bsz, seq, width), BF16),
        scratch_shapes=[pltpu.VMEM((2, tq, 1), F32), pltpu.VMEM((2, tq, 1), F32),
                        pltpu.VMEM((2, tq, hd), F32)],
        compiler_params=_cparams(("parallel", "arbitrary")),
        name="diff_attn",
    )(lam, q3, k3, v3, sz3, bias_diag, bias_adj, subln_w)


def _merge_kernel(x_ref, y_ref, u_ref, szs_ref, yd_ref, mq_ref, szm_ref, g_ref, mkv_ref,
                  dsk_ref, wglu_ref, wbs_ref, wbd_ref, wbm_ref, wout_ref, lng_ref, lnb_ref, o_ref,
                  *, width, d_model, alpha):
    nt = (((1,), (1,)), ((), ()))
    ya = jax.nn.gelu(y_ref[...] + dsk_ref[...] * u_ref[...])
    glu = jnp.dot(ya.astype(BF16), wglu_ref[...], preferred_element_type=F32)
    y_ssm = glu[:, :width] * jax.nn.sigmoid(glu[:, width:]) * szs_ref[...].astype(F32)
    merged = g_ref[:, 0:d_model].astype(F32) * jnp.dot(y_ssm.astype(BF16), wbs_ref[...],
                                                      preferred_element_type=F32)
    merged += g_ref[:, d_model:2 * d_model].astype(F32) * jnp.dot(yd_ref[...], wbd_ref[...],
                                                                  preferred_element_type=F32)
    heads = []
    for h in range(MEM_HEADS):
        cols = pl.ds(h * MEM_HEAD_DIM, MEM_HEAD_DIM)
        s = lax.dot_general(mq_ref[:, cols], mkv_ref[:, cols], nt, preferred_element_type=F32)
        p = jnp.exp(s - jnp.max(s, axis=-1, keepdims=True))
        l = jnp.sum(p, axis=-1, keepdims=True)
        o = jnp.dot(p.astype(BF16), mkv_ref[:, pl.ds(width + h * MEM_HEAD_DIM, MEM_HEAD_DIM)],
                    preferred_element_type=F32)
        heads.append(o / l)
    y_mem = jnp.concatenate(heads, axis=-1) * szm_ref[...].astype(F32)
    merged += g_ref[:, 2 * d_model:3 * d_model].astype(F32) * jnp.dot(y_mem.astype(BF16), wbm_ref[...],
                                                                      preferred_element_type=F32)
    out = jnp.dot(merged.astype(BF16), wout_ref[...], preferred_element_type=F32)
    hres = alpha * x_ref[...] + out
    mu = jnp.mean(hres, axis=-1, keepdims=True)
    cen = hres - mu
    var = jnp.mean(cen * cen, axis=-1, keepdims=True)
    o_ref[...] = cen * lax.rsqrt(var + LN_EPS) * lng_ref[...] + lnb_ref[...]


def _merge(x2, y2, u2, szs, yd, mq, szm, g, mkv, d_skip, w_glu, w_bs, w_bd, w_bm, w_out, ln_g, ln_b,
           seq, alpha):
    t, d_model = x2.shape
    width = y2.shape[1]
    n_mem = mkv.shape[1]
    tm = MERGE_TM
    per_b = seq // tm
    row = lambda n: pl.BlockSpec((tm, n), lambda i: (i, 0))
    full = lambda a: pl.BlockSpec(a.shape, lambda i: (0,) * a.ndim)
    return pl.pallas_call(
        functools.partial(_merge_kernel, width=width, d_model=d_model, alpha=alpha),
        grid=(t // tm,),
        in_specs=[row(d_model), row(width), row(width), row(width), row(width), row(width), row(width),
                  row(3 * d_model),
                  pl.BlockSpec((None, n_mem, 2 * width), lambda i: (i // per_b, 0, 0)),
                  full(d_skip), full(w_glu), full(w_bs), full(w_bd), full(w_bm), full(w_out),
                  full(ln_g), full(ln_b)],
        out_specs=row(d_model),
        out_shape=jax.ShapeDtypeStruct((t, d_model), F32),
        compiler_params=_cparams(("parallel",)),
        name="merge_out",
    )(x2, y2, u2, szs, yd, mq, szm, g, mkv, d_skip, w_glu, w_bs, w_bd, w_bm, w_out, ln_g, ln_b)


def kernel(x, mem, w_in, lam_re, lam_im, log_dt, b_re, b_im, c_re, c_im, d_skip, w_glu, lambda_q1, lambda_k1, lambda_q2, lambda_k2, subln_w, rel_bias, w_mem_kv, w_br_ssm, w_br_diff, w_br_mem, w_out, ln_g, ln_b):
    bsz, seq, d_model = x.shape
    depth = w_in.shape[0]
    width = d_model // 2
    n_mem = mem.shape[1]
    alpha = (2.0 * depth) ** 0.25
    bias_diag, bias_adj = _bias_tiles(rel_bias, seq, DIFF_TQ)
    mem2 = mem.reshape(bsz * n_mem, d_model)
    h = x.reshape(bsz * seq, d_model)
    for layer in range(depth):
        lambda_init = 0.8 - 0.6 * math.exp(-0.3 * layer)
        u, szs, dq, dk, dv, szd, mq, szm, g = _projection(h, w_in[layer].astype(BF16), width)
        mkv = _memkv(mem2, w_mem_kv[layer].astype(BF16)).reshape(bsz, n_mem, 2 * width)
        mats = _s5_matrices(lam_re[layer], lam_im[layer], log_dt[layer], b_re[layer], b_im[layer],
                            c_re[layer], c_im[layer])
        y = _s5_core(u.reshape(bsz, seq, width), *mats).reshape(bsz * seq, width)
        lam = (jnp.exp(jnp.sum(lambda_q1[layer].astype(F32) * lambda_k1[layer].astype(F32)))
               - jnp.exp(jnp.sum(lambda_q2[layer].astype(F32) * lambda_k2[layer].astype(F32)))
               + lambda_init).reshape(1)
        r3 = lambda a: a.reshape(bsz, seq, width)
        yd = _diff_attention(lam, r3(dq), r3(dk), r3(dv), r3(szd), bias_diag, bias_adj,
                             subln_w[layer].reshape(1, -1).astype(F32), lambda_init)
        h = _merge(h, y, u, szs, yd.reshape(bsz * seq, width), mq, szm, g, mkv,
                   d_skip[layer].reshape(1, -1), w_glu[layer].astype(BF16), w_br_ssm[layer].astype(BF16),
                   w_br_diff[layer].astype(BF16), w_br_mem[layer].astype(BF16), w_out[layer].astype(BF16),
                   ln_g[layer].reshape(1, -1), ln_b[layer].reshape(1, -1), seq, alpha)
    return h.reshape(bsz, seq, d_model)
```

```python
import functools
import math

import jax
import jax.numpy as jnp
from jax import lax
from jax.experimental import pallas as pl
from jax.experimental.pallas import tpu as pltpu

F32 = jnp.float32
BF16 = jnp.bfloat16

LANES = 128
VMEM_LIMIT_BYTES = 56 * 1024 * 1024

CHUNK = 64
SSM_GROUP = 16
SSM_STATE = 64
SSM_L = 16
DIFF_HEADS = 4
DIFF_HEAD_DIM = 64
MEM_HEADS = 4
MEM_HEAD_DIM = 128
REL_BUCKETS = 32
REL_MAX_DIST = 128
LN_EPS = 1e-5
RMS_EPS = 1e-5
NEG_INF = -1e30
LOG2E = math.log2(math.e)

PROJ_TM = 512
DIFF_TQ = 512
DIFF_UNIT = 128
MERGE_TM = 512


def _cparams(sem):
    return pltpu.CompilerParams(dimension_semantics=sem, vmem_limit_bytes=VMEM_LIMIT_BYTES)


def _silu(z):
    return z * jax.nn.sigmoid(z)


def _proj_kernel(x_ref, w_ref, u_ref, szs_ref, q_ref, k_ref, v_ref, szd_ref, mq_ref, szm_ref, g_ref,
                 *, width, mem_scale, diff_scale):
    xb = x_ref[...].astype(BF16)

    def mm(col, ncols):
        return jnp.dot(xb, w_ref[:, col:col + ncols], preferred_element_type=F32)

    w = width
    u_ref[...] = mm(0, w)
    szs_ref[...] = _silu(mm(w, w)).astype(BF16)
    q_ref[...] = (mm(2 * w, w) * diff_scale).astype(BF16)
    k_ref[...] = mm(3 * w, w).astype(BF16)
    v_ref[...] = mm(4 * w, w).astype(BF16)
    szd_ref[...] = _silu(mm(5 * w, w)).astype(BF16)
    mq_ref[...] = (mm(6 * w, w) * mem_scale).astype(BF16)
    szm_ref[...] = _silu(mm(7 * w, w)).astype(BF16)
    for j in range(6):
        g_ref[:, j * w:(j + 1) * w] = jax.nn.sigmoid(mm(8 * w + j * w, w)).astype(BF16)


def _projection(x2, w_bf, width):
    t, d = x2.shape
    d_in = w_bf.shape[1]
    tm = PROJ_TM
    row = lambda n: pl.BlockSpec((tm, n), lambda i: (i, 0))
    outs = [jax.ShapeDtypeStruct((t, width), F32)] + [jax.ShapeDtypeStruct((t, width), BF16)] * 7 \
        + [jax.ShapeDtypeStruct((t, 6 * width), BF16)]
    return pl.pallas_call(
        functools.partial(_proj_kernel, width=width, mem_scale=MEM_HEAD_DIM ** -0.5,
                          diff_scale=DIFF_HEAD_DIM ** -0.5 * LOG2E),
        grid=(t // tm,),
        in_specs=[row(d), pl.BlockSpec((d, d_in), lambda i: (0, 0), pipeline_mode=pl.Buffered(1))],
        out_specs=[row(width)] * 8 + [row(6 * width)],
        out_shape=outs,
        compiler_params=_cparams(("parallel",)),
        name="in_proj",
    )(x2, w_bf)


def _memkv_kernel(m_ref, w_ref, o_ref):
    o_ref[...] = jnp.dot(m_ref[...].astype(BF16), w_ref[...], preferred_element_type=F32).astype(BF16)


def _memkv(mem2, w_bf):
    r, d = mem2.shape
    n = w_bf.shape[1]
    tm = 512
    return pl.pallas_call(
        _memkv_kernel,
        grid=(r // tm,),
        in_specs=[pl.BlockSpec((tm, d), lambda i: (i, 0)), pl.BlockSpec((d, n), lambda i: (0, 0))],
        out_specs=pl.BlockSpec((tm, n), lambda i: (i, 0)),
        out_shape=jax.ShapeDtypeStruct((r, n), BF16),
        compiler_params=_cparams(("parallel",)),
        name="mem_kv",
    )(mem2, w_bf)


def _s5_matrices(lam_re, lam_im, log_dt, b_re, b_im, c_re, c_im):
    L = SSM_L
    g_n, p_n = lam_re.shape
    h_n = b_re.shape[-1]
    gl = LANES // h_n
    nb = g_n // gl
    sw = gl * p_n
    dt = jnp.exp(log_dt.astype(F32))[:, None]
    zr = lam_re.astype(F32) * dt
    zi = lam_im.astype(F32) * dt
    ks = jnp.arange(L + 1, dtype=F32)[:, None, None]
    mag = jnp.exp(zr[None] * ks)
    pr = mag * jnp.cos(zi[None] * ks)
    pi = mag * jnp.sin(zi[None] * ks)
    nr, ni = pr[1] - 1.0, pi[1]
    den = lam_re * lam_re + lam_im * lam_im
    fr = (nr * lam_re + ni * lam_im) / den
    fi = (ni * lam_re - nr * lam_im) / den
    bbr = fr[..., None] * b_re - fi[..., None] * b_im
    bbi = fr[..., None] * b_im + fi[..., None] * b_re
    gsel = jnp.arange(gl)[:, None]
    mask_s = ((jnp.arange(2 * sw)[None, :] % sw) // p_n == gsel).astype(F32)
    mask_o = ((jnp.arange(L * LANES)[None, :] % LANES) // h_n == gsel).astype(F32)

    rev = L - 1 - jnp.arange(L)
    pwl_r = pr[rev].reshape(L, g_n * p_n)
    pwl_i = pi[rev].reshape(L, g_n * p_n)
    bbl_r = bbr.transpose(2, 0, 1).reshape(h_n, g_n * p_n)
    bbl_i = bbi.transpose(2, 0, 1).reshape(h_n, g_n * p_n)
    w_r = pwl_r[:, None, :] * bbl_r[None] - pwl_i[:, None, :] * bbl_i[None]
    w_i = pwl_r[:, None, :] * bbl_i[None] + pwl_i[:, None, :] * bbl_r[None]
    blk = lambda w: w.reshape(L, h_n, nb, sw).transpose(2, 0, 1, 3)
    w_ri = jnp.concatenate([blk(w_r), blk(w_i)], axis=-1)
    bmat = (w_ri[:, :, None, :, :] * mask_s[None, None, :, None, :]).reshape(nb, L * LANES, 2 * sw)

    cl_r = c_re.transpose(2, 0, 1).reshape(p_n, g_n * h_n)
    cl_i = c_im.transpose(2, 0, 1).reshape(p_n, g_n * h_n)
    pwx_r = jnp.repeat(pr.transpose(0, 2, 1), h_n, axis=-1)
    pwx_i = jnp.repeat(pi.transpose(0, 2, 1), h_n, axis=-1)
    cp_r = cl_r[None] * pwx_r - cl_i[None] * pwx_i
    cp_i = cl_r[None] * pwx_i + cl_i[None] * pwx_r
    oblk = lambda c: c.reshape(L, p_n, nb, LANES).transpose(2, 1, 0, 3).reshape(nb, p_n, L * LANES)
    c_ri = jnp.stack([oblk(cp_r[1:]), oblk(-cp_i[1:])], axis=1)
    cmat = (c_ri[:, :, None, :, :] * mask_o[None, None, :, None, :]).reshape(nb, 2 * sw, L * LANES)

    bbx_r = jnp.repeat(bbr.transpose(2, 1, 0), h_n, axis=-1)
    bbx_i = jnp.repeat(bbi.transpose(2, 1, 0), h_n, axis=-1)
    kt = jnp.sum(cp_r[:L, None] * bbx_r[None] - cp_i[:L, None] * bbx_i[None], axis=2)
    base = kt.reshape(L, h_n, nb, LANES).transpose(2, 1, 0, 3).reshape(nb, h_n, L * LANES)
    rows = jnp.stack([jnp.pad(base, ((0, 0), (0, 0), (j * LANES, 0)))[:, :, :L * LANES] for j in range(L)],
                     axis=1)
    kmat = (rows[:, :, None, :, :] * mask_o[None, None, :, None, :]).reshape(nb, L * LANES, L * LANES)
    a_re = pr[L].reshape(nb, 1, sw)
    a_im = pi[L].reshape(nb, 1, sw)
    return kmat.astype(BF16), bmat.astype(BF16), cmat.astype(BF16), a_re, a_im


def _s5_kernel(u_ref, k_ref, b_ref, c_ref, ar_ref, ai_ref, y_ref, u2_ref, v_ref, sp_ref, y2_ref,
               *, n_chunks, n_state):
    L = SSM_L
    for i in range(L):
        u2_ref[:, i * LANES:(i + 1) * LANES] = u_ref[pl.ds(i, n_chunks, stride=L), :].astype(BF16)
    v_ref[...] = jnp.dot(u2_ref[...], b_ref[...], preferred_element_type=F32)
    ar = ar_ref[...]
    ai = ai_ref[...]

    def step(c, carry):
        sr, si = carry
        sp_ref[pl.ds(c, 1), 0:n_state] = sr
        sp_ref[pl.ds(c, 1), n_state:2 * n_state] = si
        vr = v_ref[pl.ds(c, 1), 0:n_state]
        vi = v_ref[pl.ds(c, 1), n_state:2 * n_state]
        return ar * sr - ai * si + vr, ar * si + ai * sr + vi

    zero = jnp.zeros((1, n_state), F32)
    lax.fori_loop(0, n_chunks, step, (zero, zero))
    y2_ref[...] = (jnp.dot(u2_ref[...], k_ref[...], preferred_element_type=F32)
                   + jnp.dot(sp_ref[...].astype(BF16), c_ref[...], preferred_element_type=F32))
    for i in range(L):
        y_ref[pl.ds(i, n_chunks, stride=L), :] = y2_ref[:, i * LANES:(i + 1) * LANES]


def _s5_core(u3, kmat, bmat, cmat, a_re, a_im):
    bsz, seq, width = u3.shape
    nb = width // LANES
    n_chunks = seq // SSM_L
    n_state = a_re.shape[-1]
    kdim = SSM_L * LANES
    const = lambda shape: pl.BlockSpec((None,) + shape, lambda m, b: (m, 0, 0), pipeline_mode=pl.Buffered(1))
    return pl.pallas_call(
        functools.partial(_s5_kernel, n_chunks=n_chunks, n_state=n_state),
        grid=(nb, bsz),
        in_specs=[pl.BlockSpec((None, seq, LANES), lambda m, b: (b, 0, m)),
                  const((kdim, kdim)), const((kdim, 2 * n_state)), const((2 * n_state, kdim)),
                  const((1, n_state)), const((1, n_state))],
        out_specs=pl.BlockSpec((None, seq, LANES), lambda m, b: (b, 0, m)),
        out_shape=jax.ShapeDtypeStruct((bsz, seq, width), F32),
        scratch_shapes=[pltpu.VMEM((n_chunks, kdim), BF16), pltpu.VMEM((n_chunks, 2 * n_state), F32),
                        pltpu.VMEM((n_chunks, 2 * n_state), F32), pltpu.VMEM((n_chunks, kdim), F32)],
        compiler_params=_cparams(("parallel", "parallel")),
        name="s5_core",
    )(u3, kmat, bmat, cmat, a_re, a_im)


def _t5_bucket(rel):
    half = REL_BUCKETS // 2
    max_exact = half // 2
    ret = jnp.where(rel > 0, half, 0)
    n = jnp.abs(rel)
    large = max_exact + (jnp.log(jnp.maximum(n, 1).astype(jnp.float32) / max_exact)
                         / math.log(REL_MAX_DIST / max_exact) * (half - max_exact)).astype(jnp.int32)
    large = jnp.minimum(large, half - 1)
    return ret + jnp.where(n < max_exact, n, large)


def _bias_tiles(rel_bias, seq, tq):
    assert tq >= REL_MAX_DIST and tq % CHUNK == 0

    def bias_of(rel):
        bucket = _t5_bucket(rel)
        out = jnp.zeros((rel_bias.shape[1],) + rel.shape, F32)
        for b in range(REL_BUCKETS):
            out = jnp.where((bucket == b)[None], rel_bias[b].astype(F32)[:, None, None], out)
        return out

    far = bias_of(jnp.full((1, 1), -(seq - 1), jnp.int32))
    r = jnp.arange(tq)[:, None]
    c = jnp.arange(tq)[None, :]
    diag = (bias_of(c - r) - far) * LOG2E
    diag = jnp.where(((c // CHUNK) <= (r // CHUNK))[None], diag, NEG_INF)
    adj = (bias_of(c - r[:DIFF_UNIT] - tq) - far) * LOG2E
    return diag, adj


def _diff_kernel(lam_ref, q_ref, k_ref, v_ref, z_ref, bd_ref, ba_ref, w_ref, o_ref,
                 qs_ref, m_ref, acc_ref, *, tq, out_scale):
    qi = pl.program_id(2)
    lam = lam_ref[0]
    ur = DIFF_UNIT
    n_units = tq // ur
    nt = (((1,), (1,)), ((), ()))
    lane = lax.broadcasted_iota(jnp.int32, (ur, LANES), 1)

    for u in range(n_units):
        qh = q_ref[u * ur:(u + 1) * ur, :].astype(F32)
        qs_ref[u, 0:ur, :] = jnp.where(lane < DIFF_HEAD_DIM, qh, 0.0).astype(BF16)
        qs_ref[u, ur:2 * ur, :] = jnp.where(lane >= DIFF_HEAD_DIM, qh, 0.0).astype(BF16)
    m_ref[...] = jnp.full(m_ref.shape, NEG_INF, F32)
    acc_ref[...] = jnp.zeros(acc_ref.shape, F32)

    def tile(u, k0, nk, bias):
        kt = k_ref[pl.ds(k0, nk), :]
        vt = v_ref[pl.ds(k0, nk), :]
        s = lax.dot_general(qs_ref[u], kt, nt, preferred_element_type=F32)
        if bias is not None:
            s = s + jnp.concatenate([bias, bias], axis=0)
        blocks = [s[:, j * LANES:(j + 1) * LANES] for j in range(nk // LANES)]
        m_old = m_ref[u]
        m_new = jnp.maximum(m_old, jnp.max(functools.reduce(jnp.maximum, blocks), axis=-1, keepdims=True))
        alpha = jnp.exp2(m_old - m_new)
        p = jnp.concatenate([jnp.exp2(blk - m_new) for blk in blocks], axis=1).astype(BF16)
        v_ext = jnp.concatenate([vt, jnp.ones((nk, LANES), BF16)], axis=1)
        pv = jnp.dot(p, v_ext, preferred_element_type=F32)
        acc_ref[u] = jnp.concatenate([alpha, alpha], axis=1) * acc_ref[u] + pv
        m_ref[u] = m_new

    def far(ki, carry):
        for u in range(n_units):
            tile(u, pl.multiple_of(ki * tq, tq), tq, None)
        return carry

    lax.fori_loop(0, jnp.maximum(qi - 1, 0), far, 0)

    @pl.when(qi >= 1)
    def _():
        k0 = pl.multiple_of((qi - 1) * tq, tq)
        tile(0, k0, tq, ba_ref[...])
        for u in range(1, n_units):
            tile(u, k0, tq, None)

    for u in range(n_units):
        nk = (u + 1) * ur
        tile(u, pl.multiple_of(qi * tq, tq), nk, bd_ref[u * ur:(u + 1) * ur, 0:nk])

    for u in range(n_units):
        acc = acc_ref[u]
        o = acc[0:ur, 0:LANES] / acc[0:ur, LANES:] - lam * (acc[ur:, 0:LANES] / acc[ur:, LANES:])
        o = o * lax.rsqrt(jnp.mean(o * o, axis=-1, keepdims=True) + RMS_EPS) * w_ref[...]
        o = o * out_scale
        rows = pl.ds(u * ur, ur)
        o_ref[rows, :] = (o * z_ref[rows, :].astype(F32)).astype(o_ref.dtype)


def _diff_attention(lam, q3, k3, v3, sz3, bias_diag, bias_adj, subln_w, lambda_init):
    bsz, seq, width = q3.shape
    tq = DIFF_TQ
    ur = DIFF_UNIT
    hd = 2 * DIFF_HEAD_DIM
    assert hd == LANES and ur >= REL_MAX_DIST
    qspec = pl.BlockSpec((None, tq, hd), lambda b, h, i: (b, i, h))
    kvspec = pl.BlockSpec((None, seq, hd), lambda b, h, i: (b, 0, h))
    return pl.pallas_call(
        functools.partial(_diff_kernel, tq=tq, out_scale=1.0 - lambda_init),
        grid=(bsz, width // hd, seq // tq),
        in_specs=[pl.BlockSpec(memory_space=pltpu.SMEM), qspec, kvspec, kvspec, qspec,
                  pl.BlockSpec((None, tq, tq), lambda b, h, i: (h, 0, 0)),
                  pl.BlockSpec((None, ur, tq), lambda b, h, i: (h, 0, 0)),
                  pl.BlockSpec((1, hd), lambda b, h, i: (0, 0))],
        out_specs=qspec,
        out_shape=jax.ShapeDtypeStruct((bsz, seq, width), BF16),
        scratch_shapes=[pltpu.VMEM((tq // ur, 2 * ur, hd), BF16), pltpu.VMEM((tq // ur, 2 * ur, LANES), F32),
                        pltpu.VMEM((tq // ur, 2 * ur, 2 * LANES), F32)],
        compiler_params=_cparams(("parallel", "parallel", "arbitrary")),
        name="diff_attn",
    )(lam, q3, k3, v3, sz3, bias_diag, bias_adj, subln_w)


def _merge_kernel(x_ref, y_ref, u_ref, szs_ref, yd_ref, mq_ref, szm_ref, g_ref, mkv_ref,
                  dsk_ref, wglu_ref, wbs_ref, wbd_ref, wbm_ref, wout_ref, lng_ref, lnb_ref, o_ref,
                  *, width, d_model, alpha):
    nt = (((1,), (1,)), ((), ()))
    ya = jax.nn.gelu(y_ref[...] + dsk_ref[...] * u_ref[...])
    glu = jnp.dot(ya.astype(BF16), wglu_ref[...], preferred_element_type=F32)
    y_ssm = glu[:, :width] * jax.nn.sigmoid(glu[:, width:]) * szs_ref[...].astype(F32)
    merged = g_ref[:, 0:d_model].astype(F32) * jnp.dot(y_ssm.astype(BF16), wbs_ref[...],
                                                      preferred_element_type=F32)
    merged += g_ref[:, d_model:2 * d_model].astype(F32) * jnp.dot(yd_ref[...], wbd_ref[...],
                                                                  preferred_element_type=F32)
    heads = []
    for h in range(MEM_HEADS):
        cols = pl.ds(h * MEM_HEAD_DIM, MEM_HEAD_DIM)
        s = lax.dot_general(mq_ref[:, cols], mkv_ref[:, cols], nt, preferred_element_type=F32)
        p = jnp.exp(s - jnp.max(s, axis=-1, keepdims=True))
        l = jnp.sum(p, axis=-1, keepdims=True)
        o = jnp.dot(p.astype(BF16), mkv_ref[:, pl.ds(width + h * MEM_HEAD_DIM, MEM_HEAD_DIM)],
                    preferred_element_type=F32)
        heads.append(o / l)
    y_mem = jnp.concatenate(heads, axis=-1) * szm_ref[...].astype(F32)
    merged += g_ref[:, 2 * d_model:3 * d_model].astype(F32) * jnp.dot(y_mem.astype(BF16), wbm_ref[...],
                                                                      preferred_element_type=F32)
    out = jnp.dot(merged.astype(BF16), wout_ref[...], preferred_element_type=F32)
    hres = alpha * x_ref[...] + out
    mu = jnp.mean(hres, axis=-1, keepdims=True)
    cen = hres - mu
    var = jnp.mean(cen * cen, axis=-1, keepdims=True)
    o_ref[...] = cen * lax.rsqrt(var + LN_EPS) * lng_ref[...] + lnb_ref[...]


def _merge(x2, y2, u2, szs, yd, mq, szm, g, mkv, d_skip, w_glu, w_bs, w_bd, w_bm, w_out, ln_g, ln_b,
           seq, alpha):
    t, d_model = x2.shape
    width = y2.shape[1]
    n_mem = mkv.shape[1]
    tm = MERGE_TM
    per_b = seq // tm
    row = lambda n: pl.BlockSpec((tm, n), lambda i: (i, 0))
    full = lambda a: pl.BlockSpec(a.shape, lambda i: (0,) * a.ndim)
    return pl.pallas_call(
        functools.partial(_merge_kernel, width=width, d_model=d_model, alpha=alpha),
        grid=(t // tm,),
        in_specs=[row(d_model), row(width), row(width), row(width), row(width), row(width), row(width),
                  row(3 * d_model),
                  pl.BlockSpec((None, n_mem, 2 * width), lambda i: (i // per_b, 0, 0)),
                  full(d_skip), full(w_glu), full(w_bs), full(w_bd), full(w_bm), full(w_out),
                  full(ln_g), full(ln_b)],
        out_specs=row(d_model),
        out_shape=jax.ShapeDtypeStruct((t, d_model), F32),
        compiler_params=_cparams(("parallel",)),
        name="merge_out",
    )(x2, y2, u2, szs, yd, mq, szm, g, mkv, d_skip, w_glu, w_bs, w_bd, w_bm, w_out, ln_g, ln_b)


def kernel(x, mem, w_in, lam_re, lam_im, log_dt, b_re, b_im, c_re, c_im, d_skip, w_glu, lambda_q1, lambda_k1, lambda_q2, lambda_k2, subln_w, rel_bias, w_mem_kv, w_br_ssm, w_br_diff, w_br_mem, w_out, ln_g, ln_b):
    bsz, seq, d_model = x.shape
    depth = w_in.shape[0]
    width = d_model // 2
    n_mem = mem.shape[1]
    alpha = (2.0 * depth) ** 0.25
    bias_diag, bias_adj = _bias_tiles(rel_bias, seq, DIFF_TQ)
    mem2 = mem.reshape(bsz * n_mem, d_model)
    h = x.reshape(bsz * seq, d_model)
    for layer in range(depth):
        lambda_init = 0.8 - 0.6 * math.exp(-0.3 * layer)
        u, szs, dq, dk, dv, szd, mq, szm, g = _projection(h, w_in[layer].astype(BF16), width)
        mkv = _memkv(mem2, w_mem_kv[layer].astype(BF16)).reshape(bsz, n_mem, 2 * width)
        mats = _s5_matrices(lam_re[layer], lam_im[layer], log_dt[layer], b_re[layer], b_im[layer],
                            c_re[layer], c_im[layer])
        y = _s5_core(u.reshape(bsz, seq, width), *mats).reshape(bsz * seq, width)
        lam = (jnp.exp(jnp.sum(lambda_q1[layer].astype(F32) * lambda_k1[layer].astype(F32)))
               - jnp.exp(jnp.sum(lambda_q2[layer].astype(F32) * lambda_k2[layer].astype(F32)))
               + lambda_init).reshape(1)
        r3 = lambda a: a.reshape(bsz, seq, width)
        yd = _diff_attention(lam, r3(dq), r3(dk), r3(dv), r3(szd), bias_diag, bias_adj,
                             subln_w[layer].reshape(1, -1).astype(F32), lambda_init)
        h = _merge(h, y, u, szs, yd.reshape(bsz * seq, width), mq, szm, g, mkv,
                   d_skip[layer].reshape(1, -1), w_glu[layer].astype(BF16), w_br_ssm[layer].astype(BF16),
                   w_br_diff[layer].astype(BF16), w_br_mem[layer].astype(BF16), w_out[layer].astype(BF16),
                   ln_g[layer].reshape(1, -1), ln_b[layer].reshape(1, -1), seq, alpha)
    return h.reshape(bsz, seq, d_model)
```

```python
import functools
import math

import jax
import jax.numpy as jnp
from jax import lax
from jax.experimental import pallas as pl
from jax.experimental.pallas import tpu as pltpu

F32 = jnp.float32
BF16 = jnp.bfloat16

LANES = 128
VMEM_LIMIT_BYTES = 56 * 1024 * 1024

CHUNK = 64
SSM_GROUP = 16
SSM_STATE = 64
SSM_L = 16
DIFF_HEADS = 4
DIFF_HEAD_DIM = 64
MEM_HEADS = 4
MEM_HEAD_DIM = 128
REL_BUCKETS = 32
REL_MAX_DIST = 128
LN_EPS = 1e-5
RMS_EPS = 1e-5
NEG_INF = -1e30
LOG2E = math.log2(math.e)

PROJ_TM = 512
DIFF_TQ = 512
DIFF_UNIT = 128
MERGE_TM = 512


def _cparams(sem):
    return pltpu.CompilerParams(dimension_semantics=sem, vmem_limit_bytes=VMEM_LIMIT_BYTES)


def _silu(z):
    return z * jax.nn.sigmoid(z)


def _proj_kernel(x_ref, w_ref, u_ref, szs_ref, q_ref, k_ref, v_ref, szd_ref, mq_ref, szm_ref, g_ref,
                 *, width, mem_scale, diff_scale):
    xb = x_ref[...].astype(BF16)

    def mm(col, ncols):
        return jnp.dot(xb, w_ref[:, col:col + ncols], preferred_element_type=F32)

    w = width
    u_ref[...] = mm(0, w)
    szs_ref[...] = _silu(mm(w, w)).astype(BF16)
    q_ref[...] = (mm(2 * w, w) * diff_scale).astype(BF16)
    k_ref[...] = mm(3 * w, w).astype(BF16)
    v_ref[...] = mm(4 * w, w).astype(BF16)
    szd_ref[...] = _silu(mm(5 * w, w)).astype(BF16)
    mq_ref[...] = (mm(6 * w, w) * mem_scale).astype(BF16)
    szm_ref[...] = _silu(mm(7 * w, w)).astype(BF16)
    for j in range(6):
        g_ref[:, j * w:(j + 1) * w] = jax.nn.sigmoid(mm(8 * w + j * w, w)).astype(BF16)


def _projection(x2, w_bf, width):
    t, d = x2.shape
    d_in = w_bf.shape[1]
    tm = PROJ_TM
    row = lambda n: pl.BlockSpec((tm, n), lambda i: (i, 0))
    outs = [jax.ShapeDtypeStruct((t, width), F32)] + [jax.ShapeDtypeStruct((t, width), BF16)] * 7 \
        + [jax.ShapeDtypeStruct((t, 6 * width), BF16)]
    return pl.pallas_call(
        functools.partial(_proj_kernel, width=width, mem_scale=MEM_HEAD_DIM ** -0.5,
                          diff_scale=DIFF_HEAD_DIM ** -0.5 * LOG2E),
        grid=(t // tm,),
        in_specs=[row(d), pl.BlockSpec((d, d_in), lambda i: (0, 0), pipeline_mode=pl.Buffered(1))],
        out_specs=[row(width)] * 8 + [row(6 * width)],
        out_shape=outs,
        compiler_params=_cparams(("parallel",)),
        name="in_proj",
    )(x2, w_bf)


def _memkv_kernel(m_ref, w_ref, o_ref):
    o_ref[...] = jnp.dot(m_ref[...].astype(BF16), w_ref[...], preferred_element_type=F32).astype(BF16)


def _memkv(mem2, w_bf):
    r, d = mem2.shape
    n = w_bf.shape[1]
    tm = 512
    return pl.pallas_call(
        _memkv_kernel,
        grid=(r // tm,),
        in_specs=[pl.BlockSpec((tm, d), lambda i: (i, 0)), pl.BlockSpec((d, n), lambda i: (0, 0))],
        out_specs=pl.BlockSpec((tm, n), lambda i: (i, 0)),
        out_shape=jax.ShapeDtypeStruct((r, n), BF16),
        compiler_params=_cparams(("parallel",)),
        name="mem_kv",
    )(mem2, w_bf)


def _s5_matrices(lam_re, lam_im, log_dt, b_re, b_im, c_re, c_im):
    L = SSM_L
    g_n, p_n = lam_re.shape
    h_n = b_re.shape[-1]
    gl = LANES // h_n
    nb = g_n // gl
    sw = gl * p_n
    dt = jnp.exp(log_dt.astype(F32))[:, None]
    zr = lam_re.astype(F32) * dt
    zi = lam_im.astype(F32) * dt
    ks = jnp.arange(L + 1, dtype=F32)[:, None, None]
    mag = jnp.exp(zr[None] * ks)
    pr = mag * jnp.cos(zi[None] * ks)
    pi = mag * jnp.sin(zi[None] * ks)
    nr, ni = pr[1] - 1.0, pi[1]
    den = lam_re * lam_re + lam_im * lam_im
    fr = (nr * lam_re + ni * lam_im) / den
    fi = (ni * lam_re - nr * lam_im) / den
    bbr = fr[..., None] * b_re - fi[..., None] * b_im
    bbi = fr[..., None] * b_im + fi[..., None] * b_re
    gsel = jnp.arange(gl)[:, None]
    mask_s = ((jnp.arange(2 * sw)[None, :] % sw) // p_n == gsel).astype(F32)
    mask_o = ((jnp.arange(L * LANES)[None, :] % LANES) // h_n == gsel).astype(F32)

    rev = L - 1 - jnp.arange(L)
    pwl_r = pr[rev].reshape(L, g_n * p_n)
    pwl_i = pi[rev].reshape(L, g_n * p_n)
    bbl_r = bbr.transpose(2, 0, 1).reshape(h_n, g_n * p_n)
    bbl_i = bbi.transpose(2, 0, 1).reshape(h_n, g_n * p_n)
    w_r = pwl_r[:, None, :] * bbl_r[None] - pwl_i[:, None, :] * bbl_i[None]
    w_i = pwl_r[:, None, :] * bbl_i[None] + pwl_i[:, None, :] * bbl_r[None]
    blk = lambda w: w.reshape(L, h_n, nb, sw).transpose(2, 0, 1, 3)
    w_ri = jnp.concatenate([blk(w_r), blk(w_i)], axis=-1)
    bmat = (w_ri[:, :, None, :, :] * mask_s[None, None, :, None, :]).reshape(nb, L * LANES, 2 * sw)

    cl_r = c_re.transpose(2, 0, 1).reshape(p_n, g_n * h_n)
    cl_i = c_im.transpose(2, 0, 1).reshape(p_n, g_n * h_n)
    pwx_r = jnp.repeat(pr.transpose(0, 2, 1), h_n, axis=-1)
    pwx_i = jnp.repeat(pi.transpose(0, 2, 1), h_n, axis=-1)
    cp_r = cl_r[None] * pwx_r - cl_i[None] * pwx_i
    cp_i = cl_r[None] * pwx_i + cl_i[None] * pwx_r
    oblk = lambda c: c.reshape(L, p_n, nb, LANES).transpose(2, 1, 0, 3).reshape(nb, p_n, L * LANES)
    c_ri = jnp.stack([oblk(cp_r[1:]), oblk(-cp_i[1:])], axis=1)
    cmat = (c_ri[:, :, None, :, :] * mask_o[None, None, :, None, :]).reshape(nb, 2 * sw, L * LANES)

    bbx_r = jnp.repeat(bbr.transpose(2, 1, 0), h_n, axis=-1)
    bbx_i = jnp.repeat(bbi.transpose(2, 1, 0), h_n, axis=-1)
    kt = jnp.sum(cp_r[:L, None] * bbx_r[None] - cp_i[:L, None] * bbx_i[None], axis=2)
    base = kt.reshape(L, h_n, nb, LANES).transpose(2, 1, 0, 3).reshape(nb, h_n, L * LANES)
    rows = jnp.stack([jnp.pad(base, ((0, 0), (0, 0), (j * LANES, 0)))[:, :, :L * LANES] for j in range(L)],
                     axis=1)
    kmat = (rows[:, :, None, :, :] * mask_o[None, None, :, None, :]).reshape(nb, L * LANES, L * LANES)
    a_re = pr[L].reshape(nb, 1, sw)
    a_im = pi[L].reshape(nb, 1, sw)
    return kmat.astype(BF16), bmat.astype(BF16), cmat.astype(BF16), a_re, a_im


def _s5_kernel(u_ref, k_ref, b_ref, c_ref, ar_ref, ai_ref, y_ref, u2_ref, v_ref, sp_ref, y2_ref,
               *, n_chunks, n_state):
    L = SSM_L
    for i in range(L):
        u2_ref[:, i * LANES:(i + 1) * LANES] = u_ref[pl.ds(i, n_chunks, stride=L), :].astype(BF16)
    v_ref[...] = jnp.dot(u2_ref[...], b_ref[...], preferred_element_type=F32)
    ar = ar_ref[...]
    ai = ai_ref[...]

    def step(c, carry):
        sr, si = carry
        sp_ref[pl.ds(c, 1), 0:n_state] = sr
        sp_ref[pl.ds(c, 1), n_state:2 * n_state] = si
        vr = v_ref[pl.ds(c, 1), 0:n_state]
        vi = v_ref[pl.ds(c, 1), n_state:2 * n_state]
        return ar * sr - ai * si + vr, ar * si + ai * sr + vi

    zero = jnp.zeros((1, n_state), F32)
    lax.fori_loop(0, n_chunks, step, (zero, zero))
    y2_ref[...] = (jnp.dot(u2_ref[...], k_ref[...], preferred_element_type=F32)
                   + jnp.dot(sp_ref[...].astype(BF16), c_ref[...], preferred_element_type=F32))
    for i in range(L):
        y_ref[pl.ds(i, n_chunks, stride=L), :] = y2_ref[:, i * LANES:(i + 1) * LANES]


def _s5_core(u3, kmat, bmat, cmat, a_re, a_im):
    bsz, seq, width = u3.shape
    nb = width // LANES
    n_chunks = seq // SSM_L
    n_state = a_re.shape[-1]
    kdim = SSM_L * LANES
    const = lambda shape: pl.BlockSpec((None,) + shape, lambda m, b: (m, 0, 0), pipeline_mode=pl.Buffered(1))
    return pl.pallas_call(
        functools.partial(_s5_kernel, n_chunks=n_chunks, n_state=n_state),
        grid=(nb, bsz),
        in_specs=[pl.BlockSpec((None, seq, LANES), lambda m, b: (b, 0, m)),
                  const((kdim, kdim)), const((kdim, 2 * n_state)), const((2 * n_state, kdim)),
                  const((1, n_state)), const((1, n_state))],
        out_specs=pl.BlockSpec((None, seq, LANES), lambda m, b: (b, 0, m)),
        out_shape=jax.ShapeDtypeStruct((bsz, seq, width), F32),
        scratch_shapes=[pltpu.VMEM((n_chunks, kdim), BF16), pltpu.VMEM((n_chunks, 2 * n_state), F32),
                        pltpu.VMEM((n_chunks, 2 * n_state), F32), pltpu.VMEM((n_chunks, kdim), F32)],
        compiler_params=_cparams(("parallel", "parallel")),
        name="s5_core",
    )(u3, kmat, bmat, cmat, a_re, a_im)


def _t5_bucket(rel):
    half = REL_BUCKETS // 2
    max_exact = half // 2
    ret = jnp.where(rel > 0, half, 0)
    n = jnp.abs(rel)
    large = max_exact + (jnp.log(jnp.maximum(n, 1).astype(jnp.float32) / max_exact)
                         / math.log(REL_MAX_DIST / max_exact) * (half - max_exact)).astype(jnp.int32)
    large = jnp.minimum(large, half - 1)
    return ret + jnp.where(n < max_exact, n, large)


def _bias_tiles(rel_bias, seq, tq):
    assert tq >= REL_MAX_DIST and tq % CHUNK == 0

    def bias_of(rel):
        bucket = _t5_bucket(rel)
        out = jnp.zeros((rel_bias.shape[1],) + rel.shape, F32)
        for b in range(REL_BUCKETS):
            out = jnp.where((bucket == b)[None], rel_bias[b].astype(F32)[:, None, None], out)
        return out

    far = bias_of(jnp.full((1, 1), -(seq - 1), jnp.int32))
    r = jnp.arange(tq)[:, None]
    c = jnp.arange(tq)[None, :]
    diag = (bias_of(c - r) - far) * LOG2E
    diag = jnp.where(((c // CHUNK) <= (r // CHUNK))[None], diag, NEG_INF)
    adj = (bias_of(c - r[:DIFF_UNIT] - tq) - far) * LOG2E
    return diag, adj


def _diff_kernel(lam_ref, q_ref, k_ref, v_ref, z_ref, bd_ref, ba_ref, w_ref, o_ref,
                 qs_ref, m_ref, acc_ref, s_ref, *, tq, out_scale):
    qi = pl.program_id(2)
    lam = lam_ref[0]
    ur = DIFF_UNIT
    n_units = tq // ur
    nt = (((1,), (1,)), ((), ()))
    lane = lax.broadcasted_iota(jnp.int32, (ur, LANES), 1)

    for u in range(n_units):
        qh = q_ref[u * ur:(u + 1) * ur, :].astype(F32)
        qs_ref[u, 0:ur, :] = jnp.where(lane < DIFF_HEAD_DIM, qh, 0.0).astype(BF16)
        qs_ref[u, ur:2 * ur, :] = jnp.where(lane >= DIFF_HEAD_DIM, qh, 0.0).astype(BF16)
    m_ref[...] = jnp.full(m_ref.shape, NEG_INF, F32)
    acc_ref[...] = jnp.zeros(acc_ref.shape, F32)

    def scores(u, k0, nk):
        return lax.dot_general(qs_ref[u], k_ref[pl.ds(k0, nk), :], nt, preferred_element_type=F32)

    def update(u, s, k0, nk):
        blocks = [s[:, j * LANES:(j + 1) * LANES] for j in range(nk // LANES)]
        m_old = m_ref[u]
        m_new = jnp.maximum(m_old, jnp.max(functools.reduce(jnp.maximum, blocks), axis=-1, keepdims=True))
        alpha = jnp.exp2(m_old - m_new)
        p = jnp.concatenate([jnp.exp2(blk - m_new) for blk in blocks], axis=1).astype(BF16)
        v_ext = jnp.concatenate([v_ref[pl.ds(k0, nk), :], jnp.ones((nk, LANES), BF16)], axis=1)
        pv = jnp.dot(p, v_ext, preferred_element_type=F32)
        acc_ref[u] = jnp.concatenate([alpha, alpha], axis=1) * acc_ref[u] + pv
        m_ref[u] = m_new

    def produce(u, j, slot):
        k0 = pl.multiple_of(j * tq, tq)
        s = scores(u, k0, tq)
        if u == 0:
            bias = ba_ref[...] * (j == qi - 1).astype(F32)
            s = s + jnp.concatenate([bias, bias], axis=0)
        s_ref[slot, u] = s

    def consume(u, j, slot):
        update(u, s_ref[slot, u], pl.multiple_of(j * tq, tq), tq)

    def step(j, slot):
        for u in range(n_units):
            produce(u, j + 1, 1 - slot)
            consume(u, j, slot)

    def drain(j, slot):
        for u in range(n_units):
            consume(u, j, slot)

    @pl.when(qi >= 1)
    def _():
        for u in range(n_units):
            produce(u, 0, 0)

    def pair(i, carry):
        step(2 * i, 0)
        step(2 * i + 1, 1)
        return carry

    lax.fori_loop(0, (qi - 1) // 2, pair, 0)

    @pl.when(qi % 2 == 1)
    def _():
        drain(qi - 1, 0)

    @pl.when(jnp.logical_and(qi >= 2, qi % 2 == 0))
    def _():
        step(qi - 2, 0)
        drain(qi - 1, 1)

    for u in range(n_units):
        nk = (u + 1) * ur
        k0 = pl.multiple_of(qi * tq, tq)
        bias = bd_ref[u * ur:(u + 1) * ur, 0:nk]
        update(u, scores(u, k0, nk) + jnp.concatenate([bias, bias], axis=0), k0, nk)

    for u in range(n_units):
        acc = acc_ref[u]
        o = acc[0:ur, 0:LANES] / acc[0:ur, LANES:] - lam * (acc[ur:, 0:LANES] / acc[ur:, LANES:])
        o = o * lax.rsqrt(jnp.mean(o * o, axis=-1, keepdims=True) + RMS_EPS) * w_ref[...]
        o = o * out_scale
        rows = pl.ds(u * ur, ur)
        o_ref[rows, :] = (o * z_ref[rows, :].astype(F32)).astype(o_ref.dtype)


def _diff_attention(lam, q3, k3, v3, sz3, bias_diag, bias_adj, subln_w, lambda_init):
    bsz, seq, width = q3.shape
    tq = DIFF_TQ
    ur = DIFF_UNIT
    hd = 2 * DIFF_HEAD_DIM
    assert hd == LANES and ur >= REL_MAX_DIST
    qspec = pl.BlockSpec((None, tq, hd), lambda b, h, i: (b, i, h))
    kvspec = pl.BlockSpec((None, seq, hd), lambda b, h, i: (b, 0, h))
    return pl.pallas_call(
        functools.partial(_diff_kernel, tq=tq, out_scale=1.0 - lambda_init),
        grid=(bsz, width // hd, seq // tq),
        in_specs=[pl.BlockSpec(memory_space=pltpu.SMEM), qspec, kvspec, kvspec, qspec,
                  pl.BlockSpec((None, tq, tq), lambda b, h, i: (h, 0, 0)),
                  pl.BlockSpec((None, ur, tq), lambda b, h, i: (h, 0, 0)),
                  pl.BlockSpec((1, hd), lambda b, h, i: (0, 0))],
        out_specs=qspec,
        out_shape=jax.ShapeDtypeStruct((bsz, seq, width), BF16),
        scratch_shapes=[pltpu.VMEM((tq // ur, 2 * ur, hd), BF16), pltpu.VMEM((tq // ur, 2 * ur, LANES), F32),
                        pltpu.VMEM((tq // ur, 2 * ur, 2 * LANES), F32),
                        pltpu.VMEM((2, tq // ur, 2 * ur, tq), F32)],
        compiler_params=_cparams(("parallel", "parallel", "arbitrary")),
        name="diff_attn",
    )(lam, q3, k3, v3, sz3, bias_diag, bias_adj, subln_w)


def _merge_kernel(x_ref, y_ref, u_ref, szs_ref, yd_ref, mq_ref, szm_ref, g_ref, mkv_ref,
                  dsk_ref, wglu_ref, wbs_ref, wbd_ref, wbm_ref, wout_ref, lng_ref, lnb_ref, o_ref,
                  *, width, d_model, alpha):
    nt = (((1,), (1,)), ((), ()))
    ya = jax.nn.gelu(y_ref[...] + dsk_ref[...] * u_ref[...])
    glu = jnp.dot(ya.astype(BF16), wglu_ref[...], preferred_element_type=F32)
    y_ssm = glu[:, :width] * jax.nn.sigmoid(glu[:, width:]) * szs_ref[...].astype(F32)
    merged = g_ref[:, 0:d_model].astype(F32) * jnp.dot(y_ssm.astype(BF16), wbs_ref[...],
                                                      preferred_element_type=F32)
    merged += g_ref[:, d_model:2 * d_model].astype(F32) * jnp.dot(yd_ref[...], wbd_ref[...],
                                                                  preferred_element_type=F32)
    heads = []
    for h in range(MEM_HEADS):
        cols = pl.ds(h * MEM_HEAD_DIM, MEM_HEAD_DIM)
        s = lax.dot_general(mq_ref[:, cols], mkv_ref[:, cols], nt, preferred_element_type=F32)
        p = jnp.exp(s - jnp.max(s, axis=-1, keepdims=True))
        l = jnp.sum(p, axis=-1, keepdims=True)
        o = jnp.dot(p.astype(BF16), mkv_ref[:, pl.ds(width + h * MEM_HEAD_DIM, MEM_HEAD_DIM)],
                    preferred_element_type=F32)
        heads.append(o / l)
    y_mem = jnp.concatenate(heads, axis=-1) * szm_ref[...].astype(F32)
    merged += g_ref[:, 2 * d_model:3 * d_model].astype(F32) * jnp.dot(y_mem.astype(BF16), wbm_ref[...],
                                                                      preferred_element_type=F32)
    out = jnp.dot(merged.astype(BF16), wout_ref[...], preferred_element_type=F32)
    hres = alpha * x_ref[...] + out
    mu = jnp.mean(hres, axis=-1, keepdims=True)
    cen = hres - mu
    var = jnp.mean(cen * cen, axis=-1, keepdims=True)
    o_ref[...] = cen * lax.rsqrt(var + LN_EPS) * lng_ref[...] + lnb_ref[...]


def _merge(x2, y2, u2, szs, yd, mq, szm, g, mkv, d_skip, w_glu, w_bs, w_bd, w_bm, w_out, ln_g, ln_b,
           seq, alpha):
    t, d_model = x2.shape
    width = y2.shape[1]
    n_mem = mkv.shape[1]
    tm = MERGE_TM
    per_b = seq // tm
    row = lambda n: pl.BlockSpec((tm, n), lambda i: (i, 0))
    full = lambda a: pl.BlockSpec(a.shape, lambda i: (0,) * a.ndim)
    return pl.pallas_call(
        functools.partial(_merge_kernel, width=width, d_model=d_model, alpha=alpha),
        grid=(t // tm,),
        in_specs=[row(d_model), row(width), row(width), row(width), row(width), row(width), row(width),
                  row(3 * d_model),
                  pl.BlockSpec((None, n_mem, 2 * width), lambda i: (i // per_b, 0, 0)),
                  full(d_skip), full(w_glu), full(w_bs), full(w_bd), full(w_bm), full(w_out),
                  full(ln_g), full(ln_b)],
        out_specs=row(d_model),
        out_shape=jax.ShapeDtypeStruct((t, d_model), F32),
        compiler_params=_cparams(("parallel",)),
        name="merge_out",
    )(x2, y2, u2, szs, yd, mq, szm, g, mkv, d_skip, w_glu, w_bs, w_bd, w_bm, w_out, ln_g, ln_b)


def kernel(x, mem, w_in, lam_re, lam_im, log_dt, b_re, b_im, c_re, c_im, d_skip, w_glu, lambda_q1, lambda_k1, lambda_q2, lambda_k2, subln_w, rel_bias, w_mem_kv, w_br_ssm, w_br_diff, w_br_mem, w_out, ln_g, ln_b):
    bsz, seq, d_model = x.shape
    depth = w_in.shape[0]
    width = d_model // 2
    n_mem = mem.shape[1]
    alpha = (2.0 * depth) ** 0.25
    bias_diag, bias_adj = _bias_tiles(rel_bias, seq, DIFF_TQ)
    mem2 = mem.reshape(bsz * n_mem, d_model)
    h = x.reshape(bsz * seq, d_model)
    for layer in range(depth):
        lambda_init = 0.8 - 0.6 * math.exp(-0.3 * layer)
        u, szs, dq, dk, dv, szd, mq, szm, g = _projection(h, w_in[layer].astype(BF16), width)
        mkv = _memkv(mem2, w_mem_kv[layer].astype(BF16)).reshape(bsz, n_mem, 2 * width)
        mats = _s5_matrices(lam_re[layer], lam_im[layer], log_dt[layer], b_re[layer], b_im[layer],
                            c_re[layer], c_im[layer])
        y = _s5_core(u.reshape(bsz, seq, width), *mats).reshape(bsz * seq, width)
        lam = (jnp.exp(jnp.sum(lambda_q1[layer].astype(F32) * lambda_k1[layer].astype(F32)))
               - jnp.exp(jnp.sum(lambda_q2[layer].astype(F32) * lambda_k2[layer].astype(F32)))
               + lambda_init).reshape(1)
        r3 = lambda a: a.reshape(bsz, seq, width)
        yd = _diff_attention(lam, r3(dq), r3(dk), r3(dv), r3(szd), bias_diag, bias_adj,
                             subln_w[layer].reshape(1, -1).astype(F32), lambda_init)
        h = _merge(h, y, u, szs, yd.reshape(bsz * seq, width), mq, szm, g, mkv,
                   d_skip[layer].reshape(1, -1), w_glu[layer].astype(BF16), w_br_ssm[layer].astype(BF16),
                   w_br_diff[layer].astype(BF16), w_br_mem[layer].astype(BF16), w_out[layer].astype(BF16),
                   ln_g[layer].reshape(1, -1), ln_b[layer].reshape(1, -1), seq, alpha)
    return h.reshape(bsz, seq, d_model)
```

```python
import functools
import math

import jax
import jax.numpy as jnp
from jax import lax
from jax.experimental import pallas as pl
from jax.experimental.pallas import tpu as pltpu

F32 = jnp.float32
BF16 = jnp.bfloat16

LANES = 128
VMEM_LIMIT_BYTES = 56 * 1024 * 1024

CHUNK = 64
SSM_GROUP = 16
SSM_STATE = 64
SSM_L = 16
DIFF_HEADS = 4
DIFF_HEAD_DIM = 64
MEM_HEADS = 4
MEM_HEAD_DIM = 128
REL_BUCKETS = 32
REL_MAX_DIST = 128
LN_EPS = 1e-5
RMS_EPS = 1e-5
NEG_INF = -1e30
LOG2E = math.log2(math.e)

PROJ_TM = 512
DIFF_TQ = 512
DIFF_UNIT = 128
MERGE_TM = 512


def _cparams(sem):
    return pltpu.CompilerParams(dimension_semantics=sem, vmem_limit_bytes=VMEM_LIMIT_BYTES)


def _silu(z):
    return z * jax.nn.sigmoid(z)


def _proj_kernel(x_ref, w_ref, u_ref, szs_ref, q_ref, k_ref, v_ref, szd_ref, mq_ref, szm_ref, g_ref,
                 *, width, mem_scale, diff_scale):
    xb = x_ref[...].astype(BF16)

    def mm(col, ncols):
        return jnp.dot(xb, w_ref[:, col:col + ncols], preferred_element_type=F32)

    w = width
    u_ref[...] = mm(0, w)
    szs_ref[...] = _silu(mm(w, w)).astype(BF16)
    q_ref[...] = (mm(2 * w, w) * diff_scale).astype(BF16)
    k_ref[...] = mm(3 * w, w).astype(BF16)
    v_ref[...] = mm(4 * w, w).astype(BF16)
    szd_ref[...] = _silu(mm(5 * w, w)).astype(BF16)
    mq_ref[...] = (mm(6 * w, w) * mem_scale).astype(BF16)
    szm_ref[...] = _silu(mm(7 * w, w)).astype(BF16)
    for j in range(6):
        g_ref[:, j * w:(j + 1) * w] = jax.nn.sigmoid(mm(8 * w + j * w, w)).astype(BF16)


def _projection(x2, w_bf, width):
    t, d = x2.shape
    d_in = w_bf.shape[1]
    tm = PROJ_TM
    row = lambda n: pl.BlockSpec((tm, n), lambda i: (i, 0))
    outs = [jax.ShapeDtypeStruct((t, width), F32)] + [jax.ShapeDtypeStruct((t, width), BF16)] * 7 \
        + [jax.ShapeDtypeStruct((t, 6 * width), BF16)]
    return pl.pallas_call(
        functools.partial(_proj_kernel, width=width, mem_scale=MEM_HEAD_DIM ** -0.5,
                          diff_scale=DIFF_HEAD_DIM ** -0.5 * LOG2E),
        grid=(t // tm,),
        in_specs=[row(d), pl.BlockSpec((d, d_in), lambda i: (0, 0), pipeline_mode=pl.Buffered(1))],
        out_specs=[row(width)] * 8 + [row(6 * width)],
        out_shape=outs,
        compiler_params=_cparams(("parallel",)),
        name="in_proj",
    )(x2, w_bf)


def _memkv_kernel(m_ref, w_ref, o_ref):
    o_ref[...] = jnp.dot(m_ref[...].astype(BF16), w_ref[...], preferred_element_type=F32).astype(BF16)


def _memkv(mem2, w_bf):
    r, d = mem2.shape
    n = w_bf.shape[1]
    tm = 512
    return pl.pallas_call(
        _memkv_kernel,
        grid=(r // tm,),
        in_specs=[pl.BlockSpec((tm, d), lambda i: (i, 0)), pl.BlockSpec((d, n), lambda i: (0, 0))],
        out_specs=pl.BlockSpec((tm, n), lambda i: (i, 0)),
        out_shape=jax.ShapeDtypeStruct((r, n), BF16),
        compiler_params=_cparams(("parallel",)),
        name="mem_kv",
    )(mem2, w_bf)


def _s5_matrices(lam_re, lam_im, log_dt, b_re, b_im, c_re, c_im):
    L = SSM_L
    g_n, p_n = lam_re.shape
    h_n = b_re.shape[-1]
    gl = LANES // h_n
    nb = g_n // gl
    sw = gl * p_n
    dt = jnp.exp(log_dt.astype(F32))[:, None]
    zr = lam_re.astype(F32) * dt
    zi = lam_im.astype(F32) * dt
    ks = jnp.arange(L + 1, dtype=F32)[:, None, None]
    mag = jnp.exp(zr[None] * ks)
    pr = mag * jnp.cos(zi[None] * ks)
    pi = mag * jnp.sin(zi[None] * ks)
    nr, ni = pr[1] - 1.0, pi[1]
    den = lam_re * lam_re + lam_im * lam_im
    fr = (nr * lam_re + ni * lam_im) / den
    fi = (ni * lam_re - nr * lam_im) / den
    bbr = fr[..., None] * b_re - fi[..., None] * b_im
    bbi = fr[..., None] * b_im + fi[..., None] * b_re
    gsel = jnp.arange(gl)[:, None]
    mask_s = ((jnp.arange(2 * sw)[None, :] % sw) // p_n == gsel).astype(F32)
    mask_o = ((jnp.arange(L * LANES)[None, :] % LANES) // h_n == gsel).astype(F32)

    rev = L - 1 - jnp.arange(L)
    pwl_r = pr[rev].reshape(L, g_n * p_n)
    pwl_i = pi[rev].reshape(L, g_n * p_n)
    bbl_r = bbr.transpose(2, 0, 1).reshape(h_n, g_n * p_n)
    bbl_i = bbi.transpose(2, 0, 1).reshape(h_n, g_n * p_n)
    w_r = pwl_r[:, None, :] * bbl_r[None] - pwl_i[:, None, :] * bbl_i[None]
    w_i = pwl_r[:, None, :] * bbl_i[None] + pwl_i[:, None, :] * bbl_r[None]
    blk = lambda w: w.reshape(L, h_n, nb, sw).transpose(2, 0, 1, 3)
    w_ri = jnp.concatenate([blk(w_r), blk(w_i)], axis=-1)
    bmat = (w_ri[:, :, None, :, :] * mask_s[None, None, :, None, :]).reshape(nb, L * LANES, 2 * sw)

    cl_r = c_re.transpose(2, 0, 1).reshape(p_n, g_n * h_n)
    cl_i = c_im.transpose(2, 0, 1).reshape(p_n, g_n * h_n)
    pwx_r = jnp.repeat(pr.transpose(0, 2, 1), h_n, axis=-1)
    pwx_i = jnp.repeat(pi.transpose(0, 2, 1), h_n, axis=-1)
    cp_r = cl_r[None] * pwx_r - cl_i[None] * pwx_i
    cp_i = cl_r[None] * pwx_i + cl_i[None] * pwx_r
    oblk = lambda c: c.reshape(L, p_n, nb, LANES).transpose(2, 1, 0, 3).reshape(nb, p_n, L * LANES)
    c_ri = jnp.stack([oblk(cp_r[1:]), oblk(-cp_i[1:])], axis=1)
    cmat = (c_ri[:, :, None, :, :] * mask_o[None, None, :, None, :]).reshape(nb, 2 * sw, L * LANES)

    bbx_r = jnp.repeat(bbr.transpose(2, 1, 0), h_n, axis=-1)
    bbx_i = jnp.repeat(bbi.transpose(2, 1, 0), h_n, axis=-1)
    kt = jnp.sum(cp_r[:L, None] * bbx_r[None] - cp_i[:L, None] * bbx_i[None], axis=2)
    base = kt.reshape(L, h_n, nb, LANES).transpose(2, 1, 0, 3).reshape(nb, h_n, L * LANES)
    rows = jnp.stack([jnp.pad(base, ((0, 0), (0, 0), (j * LANES, 0)))[:, :, :L * LANES] for j in range(L)],
                     axis=1)
    kmat = (rows[:, :, None, :, :] * mask_o[None, None, :, None, :]).reshape(nb, L * LANES, L * LANES)
    a_re = pr[L].reshape(nb, 1, sw)
    a_im = pi[L].reshape(nb, 1, sw)
    return kmat.astype(BF16), bmat.astype(BF16), cmat.astype(BF16), a_re, a_im


def _s5_kernel(u_ref, k_ref, b_ref, c_ref, ar_ref, ai_ref, y_ref, u2_ref, v_ref, sp_ref, y2_ref,
               *, n_chunks, n_state):
    L = SSM_L
    for i in range(L):
        u2_ref[:, i * LANES:(i + 1) * LANES] = u_ref[pl.ds(i, n_chunks, stride=L), :].astype(BF16)
    v_ref[...] = jnp.dot(u2_ref[...], b_ref[...], preferred_element_type=F32)
    ar = ar_ref[...]
    ai = ai_ref[...]

    def step(c, carry):
        sr, si = carry
        sp_ref[pl.ds(c, 1), 0:n_state] = sr
        sp_ref[pl.ds(c, 1), n_state:2 * n_state] = si
        vr = v_ref[pl.ds(c, 1), 0:n_state]
        vi = v_ref[pl.ds(c, 1), n_state:2 * n_state]
        return ar * sr - ai * si + vr, ar * si + ai * sr + vi

    zero = jnp.zeros((1, n_state), F32)
    lax.fori_loop(0, n_chunks, step, (zero, zero))
    spb = sp_ref[...].astype(BF16)
    wb = 2 * LANES
    for blk in range(L * LANES // wb):
        cols = slice(blk * wb, (blk + 1) * wb)
        rows = (blk + 1) * wb
        y2_ref[:, cols] = (jnp.dot(u2_ref[:, 0:rows], k_ref[0:rows, cols], preferred_element_type=F32)
                           + jnp.dot(spb, c_ref[:, cols], preferred_element_type=F32))
    for i in range(L):
        y_ref[pl.ds(i, n_chunks, stride=L), :] = y2_ref[:, i * LANES:(i + 1) * LANES]


def _s5_core(u3, kmat, bmat, cmat, a_re, a_im):
    bsz, seq, width = u3.shape
    nb = width // LANES
    n_chunks = seq // SSM_L
    n_state = a_re.shape[-1]
    kdim = SSM_L * LANES
    const = lambda shape: pl.BlockSpec((None,) + shape, lambda m, b: (m, 0, 0))
    return pl.pallas_call(
        functools.partial(_s5_kernel, n_chunks=n_chunks, n_state=n_state),
        grid=(nb, bsz),
        in_specs=[pl.BlockSpec((None, seq, LANES), lambda m, b: (b, 0, m)),
                  const((kdim, kdim)), const((kdim, 2 * n_state)), const((2 * n_state, kdim)),
                  const((1, n_state)), const((1, n_state))],
        out_specs=pl.BlockSpec((None, seq, LANES), lambda m, b: (b, 0, m)),
        out_shape=jax.ShapeDtypeStruct((bsz, seq, width), F32),
        scratch_shapes=[pltpu.VMEM((n_chunks, kdim), BF16), pltpu.VMEM((n_chunks, 2 * n_state), F32),
                        pltpu.VMEM((n_chunks, 2 * n_state), F32), pltpu.VMEM((n_chunks, kdim), F32)],
        compiler_params=_cparams(("parallel", "parallel")),
        name="s5_core",
    )(u3, kmat, bmat, cmat, a_re, a_im)


def _t5_bucket(rel):
    half = REL_BUCKETS // 2
    max_exact = half // 2
    ret = jnp.where(rel > 0, half, 0)
    n = jnp.abs(rel)
    large = max_exact + (jnp.log(jnp.maximum(n, 1).astype(jnp.float32) / max_exact)
                         / math.log(REL_MAX_DIST / max_exact) * (half - max_exact)).astype(jnp.int32)
    large = jnp.minimum(large, half - 1)
    return ret + jnp.where(n < max_exact, n, large)


def _bias_tiles(rel_bias, seq, tq):
    assert tq >= REL_MAX_DIST and tq % CHUNK == 0

    def bias_of(rel):
        bucket = _t5_bucket(rel)
        out = jnp.zeros((rel_bias.shape[1],) + rel.shape, F32)
        for b in range(REL_BUCKETS):
            out = jnp.where((bucket == b)[None], rel_bias[b].astype(F32)[:, None, None], out)
        return out

    far = bias_of(jnp.full((1, 1), -(seq - 1), jnp.int32))
    r = jnp.arange(tq)[:, None]
    c = jnp.arange(tq)[None, :]
    diag = (bias_of(c - r) - far) * LOG2E
    diag = jnp.where(((c // CHUNK) <= (r // CHUNK))[None], diag, NEG_INF)
    adj = (bias_of(c - r[:DIFF_UNIT] - tq) - far) * LOG2E
    return diag, adj


def _diff_kernel(lam_ref, q_ref, k_ref, v_ref, z_ref, bd_ref, ba_ref, w_ref, o_ref,
                 qs_ref, m_ref, acc_ref, s_ref, *, tq, out_scale):
    qi = pl.program_id(2)
    lam = lam_ref[0]
    ur = DIFF_UNIT
    n_units = tq // ur
    nt = (((1,), (1,)), ((), ()))
    lane = lax.broadcasted_iota(jnp.int32, (ur, LANES), 1)

    for u in range(n_units):
        qh = q_ref[u * ur:(u + 1) * ur, :].astype(F32)
        qs_ref[u, 0:ur, :] = jnp.where(lane < DIFF_HEAD_DIM, qh, 0.0).astype(BF16)
        qs_ref[u, ur:2 * ur, :] = jnp.where(lane >= DIFF_HEAD_DIM, qh, 0.0).astype(BF16)
    m_ref[...] = jnp.full(m_ref.shape, NEG_INF, F32)
    acc_ref[...] = jnp.zeros(acc_ref.shape, F32)

    def scores(u, k0, nk):
        return lax.dot_general(qs_ref[u], k_ref[pl.ds(k0, nk), :], nt, preferred_element_type=F32)

    def update(u, s, k0, nk):
        blocks = [s[:, j * LANES:(j + 1) * LANES] for j in range(nk // LANES)]
        m_old = m_ref[u]
        m_new = jnp.maximum(m_old, jnp.max(functools.reduce(jnp.maximum, blocks), axis=-1, keepdims=True))
        alpha = jnp.exp2(m_old - m_new)
        p = jnp.concatenate([jnp.exp2(blk - m_new) for blk in blocks], axis=1).astype(BF16)
        v_ext = jnp.concatenate([v_ref[pl.ds(k0, nk), :], jnp.ones((nk, LANES), BF16)], axis=1)
        pv = jnp.dot(p, v_ext, preferred_element_type=F32)
        acc_ref[u] = jnp.concatenate([alpha, alpha], axis=1) * acc_ref[u] + pv
        m_ref[u] = m_new

    def produce(u, j, slot):
        k0 = pl.multiple_of(j * tq, tq)
        s = scores(u, k0, tq)
        if u == 0:
            bias = ba_ref[...] * (j == qi - 1).astype(F32)
            s = s + jnp.concatenate([bias, bias], axis=0)
        s_ref[slot, u] = s

    def consume(u, j, slot):
        update(u, s_ref[slot, u], pl.multiple_of(j * tq, tq), tq)

    def step(j, slot):
        for u in range(n_units):
            produce(u, j + 1, 1 - slot)
            consume(u, j, slot)

    kd = pl.multiple_of(qi * tq, tq)
    s_diag = []
    for u in range(n_units):
        nk = (u + 1) * ur
        bias = bd_ref[u * ur:(u + 1) * ur, 0:nk]
        s_diag.append(scores(u, kd, nk) + jnp.concatenate([bias, bias], axis=0))
    for u in range(n_units):
        produce(u, 0, 0)
    for u in range(n_units):
        update(u, s_diag[u], kd, (u + 1) * ur)

    def pair(i, carry):
        step(2 * i, 0)
        step(2 * i + 1, 1)
        return carry

    lax.fori_loop(0, (qi - 1) // 2, pair, 0)

    def finalize(u):
        acc = acc_ref[u]
        o = acc[0:ur, 0:LANES] / acc[0:ur, LANES:] - lam * (acc[ur:, 0:LANES] / acc[ur:, LANES:])
        o = o * lax.rsqrt(jnp.mean(o * o, axis=-1, keepdims=True) + RMS_EPS) * w_ref[...]
        o = o * out_scale
        rows = pl.ds(u * ur, ur)
        o_ref[rows, :] = (o * z_ref[rows, :].astype(F32)).astype(o_ref.dtype)

    def drain(j, slot):
        for u in range(n_units):
            consume(u, j, slot)
            finalize(u)

    @pl.when(qi == 0)
    def _():
        for u in range(n_units):
            finalize(u)

    @pl.when(qi % 2 == 1)
    def _():
        drain(qi - 1, 0)

    @pl.when(jnp.logical_and(qi >= 2, qi % 2 == 0))
    def _():
        step(qi - 2, 0)
        drain(qi - 1, 1)


def _diff_attention(lam, q3, k3, v3, sz3, bias_diag, bias_adj, subln_w, lambda_init):
    bsz, seq, width = q3.shape
    tq = DIFF_TQ
    ur = DIFF_UNIT
    hd = 2 * DIFF_HEAD_DIM
    assert hd == LANES and ur >= REL_MAX_DIST
    qspec = pl.BlockSpec((None, tq, hd), lambda b, h, i: (b, i, h))
    kvspec = pl.BlockSpec((None, seq, hd), lambda b, h, i: (b, 0, h))
    return pl.pallas_call(
        functools.partial(_diff_kernel, tq=tq, out_scale=1.0 - lambda_init),
        grid=(bsz, width // hd, seq // tq),
        in_specs=[pl.BlockSpec(memory_space=pltpu.SMEM), qspec, kvspec, kvspec, qspec,
                  pl.BlockSpec((None, tq, tq), lambda b, h, i: (h, 0, 0)),
                  pl.BlockSpec((None, ur, tq), lambda b, h, i: (h, 0, 0)),
                  pl.BlockSpec((1, hd), lambda b, h, i: (0, 0))],
        out_specs=qspec,
        out_shape=jax.ShapeDtypeStruct((bsz, seq, width), BF16),
        scratch_shapes=[pltpu.VMEM((tq // ur, 2 * ur, hd), BF16), pltpu.VMEM((tq // ur, 2 * ur, LANES), F32),
                        pltpu.VMEM((tq // ur, 2 * ur, 2 * LANES), F32),
                        pltpu.VMEM((2, tq // ur, 2 * ur, tq), F32)],
        compiler_params=_cparams(("parallel", "parallel", "arbitrary")),
        name="diff_attn",
    )(lam, q3, k3, v3, sz3, bias_diag, bias_adj, subln_w)


def _merge_kernel(x_ref, y_ref, u_ref, szs_ref, yd_ref, mq_ref, szm_ref, g_ref, mkv_ref,
                  dsk_ref, wglu_ref, wbs_ref, wbd_ref, wbm_ref, wout_ref, lng_ref, lnb_ref, o_ref,
                  *, width, d_model, alpha):
    nt = (((1,), (1,)), ((), ()))
    ya = jax.nn.gelu(y_ref[...] + dsk_ref[...] * u_ref[...])
    glu = jnp.dot(ya.astype(BF16), wglu_ref[...], preferred_element_type=F32)
    y_ssm = glu[:, :width] * jax.nn.sigmoid(glu[:, width:]) * szs_ref[...].astype(F32)
    merged = g_ref[:, 0:d_model].astype(F32) * jnp.dot(y_ssm.astype(BF16), wbs_ref[...],
                                                      preferred_element_type=F32)
    merged += g_ref[:, d_model:2 * d_model].astype(F32) * jnp.dot(yd_ref[...], wbd_ref[...],
                                                                  preferred_element_type=F32)
    heads = []
    for h in range(MEM_HEADS):
        cols = pl.ds(h * MEM_HEAD_DIM, MEM_HEAD_DIM)
        s = lax.dot_general(mq_ref[:, cols], mkv_ref[:, cols], nt, preferred_element_type=F32)
        p = jnp.exp(s - jnp.max(s, axis=-1, keepdims=True))
        l = jnp.sum(p, axis=-1, keepdims=True)
        o = jnp.dot(p.astype(BF16), mkv_ref[:, pl.ds(width + h * MEM_HEAD_DIM, MEM_HEAD_DIM)],
                    preferred_element_type=F32)
        heads.append(o / l)
    y_mem = jnp.concatenate(heads, axis=-1) * szm_ref[...].astype(F32)
    merged += g_ref[:, 2 * d_model:3 * d_model].astype(F32) * jnp.dot(y_mem.astype(BF16), wbm_ref[...],
                                                                      preferred_element_type=F32)
    out = jnp.dot(merged.astype(BF16), wout_ref[...], preferred_element_type=F32)
    hres = alpha * x_ref[...] + out
    mu = jnp.mean(hres, axis=-1, keepdims=True)
    cen = hres - mu
    var = jnp.mean(cen * cen, axis=-1, keepdims=True)
    o_ref[...] = cen * lax.rsqrt(var + LN_EPS) * lng_ref[...] + lnb_ref[...]


def _merge(x2, y2, u2, szs, yd, mq, szm, g, mkv, d_skip, w_glu, w_bs, w_bd, w_bm, w_out, ln_g, ln_b,
           seq, alpha):
    t, d_model = x2.shape
    width = y2.shape[1]
    n_mem = mkv.shape[1]
    tm = MERGE_TM
    per_b = seq // tm
    row = lambda n: pl.BlockSpec((tm, n), lambda i: (i, 0))
    full = lambda a: pl.BlockSpec(a.shape, lambda i: (0,) * a.ndim)
    return pl.pallas_call(
        functools.partial(_merge_kernel, width=width, d_model=d_model, alpha=alpha),
        grid=(t // tm,),
        in_specs=[row(d_model), row(width), row(width), row(width), row(width), row(width), row(width),
                  row(3 * d_model),
                  pl.BlockSpec((None, n_mem, 2 * width), lambda i: (i // per_b, 0, 0)),
                  full(d_skip), full(w_glu), full(w_bs), full(w_bd), full(w_bm), full(w_out),
                  full(ln_g), full(ln_b)],
        out_specs=row(d_model),
        out_shape=jax.ShapeDtypeStruct((t, d_model), F32),
        compiler_params=_cparams(("parallel",)),
        name="merge_out",
    )(x2, y2, u2, szs, yd, mq, szm, g, mkv, d_skip, w_glu, w_bs, w_bd, w_bm, w_out, ln_g, ln_b)


def kernel(x, mem, w_in, lam_re, lam_im, log_dt, b_re, b_im, c_re, c_im, d_skip, w_glu, lambda_q1, lambda_k1, lambda_q2, lambda_k2, subln_w, rel_bias, w_mem_kv, w_br_ssm, w_br_diff, w_br_mem, w_out, ln_g, ln_b):
    bsz, seq, d_model = x.shape
    depth = w_in.shape[0]
    width = d_model // 2
    n_mem = mem.shape[1]
    alpha = (2.0 * depth) ** 0.25
    bias_diag, bias_adj = _bias_tiles(rel_bias, seq, DIFF_TQ)
    mem2 = mem.reshape(bsz * n_mem, d_model)
    h = x.reshape(bsz * seq, d_model)
    for layer in range(depth):
        lambda_init = 0.8 - 0.6 * math.exp(-0.3 * layer)
        u, szs, dq, dk, dv, szd, mq, szm, g = _projection(h, w_in[layer].astype(BF16), width)
        mkv = _memkv(mem2, w_mem_kv[layer].astype(BF16)).reshape(bsz, n_mem, 2 * width)
        mats = _s5_matrices(lam_re[layer], lam_im[layer], log_dt[layer], b_re[layer], b_im[layer],
                            c_re[layer], c_im[layer])
        y = _s5_core(u.reshape(bsz, seq, width), *mats).reshape(bsz * seq, width)
        lam = (jnp.exp(jnp.sum(lambda_q1[layer].astype(F32) * lambda_k1[layer].astype(F32)))
               - jnp.exp(jnp.sum(lambda_q2[layer].astype(F32) * lambda_k2[layer].astype(F32)))
               + lambda_init).reshape(1)
        r3 = lambda a: a.reshape(bsz, seq, width)
        yd = _diff_attention(lam, r3(dq), r3(dk), r3(dv), r3(szd), bias_diag, bias_adj,
                             subln_w[layer].reshape(1, -1).astype(F32), lambda_init)
        h = _merge(h, y, u, szs, yd.reshape(bsz * seq, width), mq, szm, g, mkv,
                   d_skip[layer].reshape(1, -1), w_glu[layer].astype(BF16), w_br_ssm[layer].astype(BF16),
                   w_br_diff[layer].astype(BF16), w_br_mem[layer].astype(BF16), w_out[layer].astype(BF16),
                   ln_g[layer].reshape(1, -1), ln_b[layer].reshape(1, -1), seq, alpha)
    return h.reshape(bsz, seq, d_model)
```

```python
import functools
import math

import jax
import jax.numpy as jnp
from jax import lax
from jax.experimental import pallas as pl
from jax.experimental.pallas import tpu as pltpu

F32 = jnp.float32
BF16 = jnp.bfloat16

LANES = 128
VMEM_LIMIT_BYTES = 56 * 1024 * 1024

CHUNK = 64
SSM_GROUP = 16
SSM_STATE = 64
SSM_L = 16
DIFF_HEADS = 4
DIFF_HEAD_DIM = 64
MEM_HEADS = 4
MEM_HEAD_DIM = 128
REL_BUCKETS = 32
REL_MAX_DIST = 128
LN_EPS = 1e-5
RMS_EPS = 1e-5
NEG_INF = -1e30
LOG2E = math.log2(math.e)

PROJ_TM = 512
DIFF_TQ = 512
DIFF_UNIT = 128
MERGE_TM = 512
MERGE_GROUPS = 2


def _cparams(sem):
    return pltpu.CompilerParams(dimension_semantics=sem, vmem_limit_bytes=VMEM_LIMIT_BYTES)


def _silu(z):
    return z * jax.nn.sigmoid(z)


def _proj_kernel(x_ref, w_ref, u_ref, szs_ref, q_ref, k_ref, v_ref, szd_ref, mq_ref, szm_ref, g_ref,
                 *, width, mem_scale, diff_scale):
    xb = x_ref[...].astype(BF16)

    def mm(col, ncols):
        return jnp.dot(xb, w_ref[:, col:col + ncols], preferred_element_type=F32)

    w = width
    u_ref[...] = mm(0, w)
    szs_ref[...] = _silu(mm(w, w)).astype(BF16)
    q_ref[...] = (mm(2 * w, w) * diff_scale).astype(BF16)
    k_ref[...] = mm(3 * w, w).astype(BF16)
    v_ref[...] = mm(4 * w, w).astype(BF16)
    szd_ref[...] = _silu(mm(5 * w, w)).astype(BF16)
    mq_ref[...] = (mm(6 * w, w) * mem_scale).astype(BF16)
    szm_ref[...] = _silu(mm(7 * w, w)).astype(BF16)
    for j in range(6):
        g_ref[:, j * w:(j + 1) * w] = jax.nn.sigmoid(mm(8 * w + j * w, w)).astype(BF16)


def _projection(x2, w_bf, width):
    t, d = x2.shape
    d_in = w_bf.shape[1]
    tm = PROJ_TM
    row = lambda n: pl.BlockSpec((tm, n), lambda i: (i, 0))
    outs = [jax.ShapeDtypeStruct((t, width), F32)] + [jax.ShapeDtypeStruct((t, width), BF16)] * 7 \
        + [jax.ShapeDtypeStruct((t, 6 * width), BF16)]
    return pl.pallas_call(
        functools.partial(_proj_kernel, width=width, mem_scale=MEM_HEAD_DIM ** -0.5,
                          diff_scale=DIFF_HEAD_DIM ** -0.5 * LOG2E),
        grid=(t // tm,),
        in_specs=[row(d), pl.BlockSpec((d, d_in), lambda i: (0, 0), pipeline_mode=pl.Buffered(1))],
        out_specs=[row(width)] * 8 + [row(6 * width)],
        out_shape=outs,
        compiler_params=_cparams(("parallel",)),
        name="in_proj",
    )(x2, w_bf)


def _memkv_kernel(m_ref, w_ref, o_ref):
    o_ref[...] = jnp.dot(m_ref[...].astype(BF16), w_ref[...], preferred_element_type=F32).astype(BF16)


def _memkv(mem2, w_bf):
    r, d = mem2.shape
    n = w_bf.shape[1]
    tm = 512
    return pl.pallas_call(
        _memkv_kernel,
        grid=(r // tm,),
        in_specs=[pl.BlockSpec((tm, d), lambda i: (i, 0)), pl.BlockSpec((d, n), lambda i: (0, 0))],
        out_specs=pl.BlockSpec((tm, n), lambda i: (i, 0)),
        out_shape=jax.ShapeDtypeStruct((r, n), BF16),
        compiler_params=_cparams(("parallel",)),
        name="mem_kv",
    )(mem2, w_bf)


def _s5_matrices(lam_re, lam_im, log_dt, b_re, b_im, c_re, c_im):
    L = SSM_L
    g_n, p_n = lam_re.shape
    h_n = b_re.shape[-1]
    gl = LANES // h_n
    nb = g_n // gl
    sw = gl * p_n
    dt = jnp.exp(log_dt.astype(F32))[:, None]
    zr = lam_re.astype(F32) * dt
    zi = lam_im.astype(F32) * dt
    ks = jnp.arange(L + 1, dtype=F32)[:, None, None]
    mag = jnp.exp(zr[None] * ks)
    pr = mag * jnp.cos(zi[None] * ks)
    pi = mag * jnp.sin(zi[None] * ks)
    nr, ni = pr[1] - 1.0, pi[1]
    den = lam_re * lam_re + lam_im * lam_im
    fr = (nr * lam_re + ni * lam_im) / den
    fi = (ni * lam_re - nr * lam_im) / den
    bbr = fr[..., None] * b_re - fi[..., None] * b_im
    bbi = fr[..., None] * b_im + fi[..., None] * b_re

    rev = L - 1 - jnp.arange(L)
    pwl_r = pr[rev].reshape(L, g_n * p_n)
    pwl_i = pi[rev].reshape(L, g_n * p_n)
    bbl_r = bbr.transpose(2, 0, 1).reshape(h_n, g_n * p_n)
    bbl_i = bbi.transpose(2, 0, 1).reshape(h_n, g_n * p_n)
    w_r = pwl_r[:, None, :] * bbl_r[None] - pwl_i[:, None, :] * bbl_i[None]
    w_i = pwl_r[:, None, :] * bbl_i[None] + pwl_i[:, None, :] * bbl_r[None]
    blk = lambda w: w.reshape(L, h_n, nb, sw).transpose(2, 0, 1, 3)
    w_ri = jnp.concatenate([blk(w_r), blk(w_i)], axis=-1).reshape(nb, L * h_n, 2 * sw)

    cl_r = c_re.transpose(2, 0, 1).reshape(p_n, g_n * h_n)
    cl_i = c_im.transpose(2, 0, 1).reshape(p_n, g_n * h_n)
    pwx_r = jnp.repeat(pr.transpose(0, 2, 1), h_n, axis=-1)
    pwx_i = jnp.repeat(pi.transpose(0, 2, 1), h_n, axis=-1)
    cp_r = cl_r[None] * pwx_r - cl_i[None] * pwx_i
    cp_i = cl_r[None] * pwx_i + cl_i[None] * pwx_r
    oblk = lambda c: c.reshape(L, p_n, nb, LANES).transpose(2, 1, 0, 3).reshape(nb, p_n, L * LANES)
    c_ri = jnp.concatenate([oblk(cp_r[1:]), oblk(-cp_i[1:])], axis=1)

    bbx_r = jnp.repeat(bbr.transpose(2, 1, 0), h_n, axis=-1)
    bbx_i = jnp.repeat(bbi.transpose(2, 1, 0), h_n, axis=-1)
    kt = jnp.sum(cp_r[:L, None] * bbx_r[None] - cp_i[:L, None] * bbx_i[None], axis=2)
    base = kt.reshape(L, h_n, nb, LANES).transpose(2, 1, 0, 3).reshape(nb, h_n, L * LANES)
    a_re = pr[L].reshape(nb, 1, sw)
    a_im = pi[L].reshape(nb, 1, sw)
    return base, w_ri, c_ri, a_re, a_im


def _s5_expand(base_ref, w_ref, cf_ref, k_ref, b_ref, c_ref, *, n_state):
    L = SSM_L
    h_n = base_ref.shape[0]
    gl = LANES // h_n
    p_n = cf_ref.shape[0] // 2
    assert h_n & (h_n - 1) == 0 and p_n & (p_n - 1) == 0 and n_state & (n_state - 1) == 0
    lane_o = lax.broadcasted_iota(jnp.int32, (1, L * LANES), 1)
    lane_s = lax.broadcasted_iota(jnp.int32, (1, 2 * n_state), 1)
    group_o = jnp.right_shift(jnp.bitwise_and(lane_o, LANES - 1), h_n.bit_length() - 1)
    group_s = jnp.right_shift(jnp.bitwise_and(lane_s, n_state - 1), p_n.bit_length() - 1)
    for g in range(gl):
        base_g = jnp.where(group_o == g, base_ref[...], 0.0)
        for j in range(L):
            rows = pl.ds((j * gl + g) * h_n, h_n)
            if j > 0:
                k_ref[rows, 0:j * LANES] = jnp.zeros((h_n, j * LANES), BF16)
            k_ref[rows, j * LANES:L * LANES] = base_g[:, 0:(L - j) * LANES].astype(BF16)
            b_ref[rows, :] = jnp.where(group_s == g, w_ref[j * h_n:(j + 1) * h_n, :], 0.0).astype(BF16)
        for r in range(2):
            c_ref[pl.ds((r * gl + g) * p_n, p_n), :] = jnp.where(
                group_o == g, cf_ref[r * p_n:(r + 1) * p_n, :], 0.0).astype(BF16)


def _s5_kernel(u_ref, base_ref, w_ref, cf_ref, ar_ref, ai_ref, y_ref,
               u2_ref, v_ref, sp_ref, y2_ref, k_ref, b_ref, c_ref, *, n_chunks, n_state):
    L = SSM_L

    @pl.when(pl.program_id(1) == 0)
    def _():
        _s5_expand(base_ref, w_ref, cf_ref, k_ref, b_ref, c_ref, n_state=n_state)

    for i in range(L):
        u2_ref[:, i * LANES:(i + 1) * LANES] = u_ref[pl.ds(i, n_chunks, stride=L), :].astype(BF16)
    v_ref[...] = jnp.dot(u2_ref[...], b_ref[...], preferred_element_type=F32)
    ar = ar_ref[...]
    ai = ai_ref[...]

    def step(c, carry):
        sr, si = carry
        sp_ref[pl.ds(c, 1), 0:n_state] = sr
        sp_ref[pl.ds(c, 1), n_state:2 * n_state] = si
        vr = v_ref[pl.ds(c, 1), 0:n_state]
        vi = v_ref[pl.ds(c, 1), n_state:2 * n_state]
        return ar * sr - ai * si + vr, ar * si + ai * sr + vi

    zero = jnp.zeros((1, n_state), F32)
    lax.fori_loop(0, n_chunks, step, (zero, zero))
    spb = sp_ref[...].astype(BF16)
    wb = 2 * LANES
    for blk in range(L * LANES // wb):
        cols = slice(blk * wb, (blk + 1) * wb)
        rows = (blk + 1) * wb
        y2_ref[:, cols] = (jnp.dot(u2_ref[:, 0:rows], k_ref[0:rows, cols], preferred_element_type=F32)
                           + jnp.dot(spb, c_ref[:, cols], preferred_element_type=F32))
    for i in range(L):
        y_ref[pl.ds(i, n_chunks, stride=L), :] = y2_ref[:, i * LANES:(i + 1) * LANES]


def _s5_core(u3, base, w_ri, c_ri, a_re, a_im):
    bsz, seq, width = u3.shape
    nb = width // LANES
    n_chunks = seq // SSM_L
    n_state = a_re.shape[-1]
    kdim = SSM_L * LANES
    const = lambda a: pl.BlockSpec((None,) + a.shape[1:], lambda m, b: (m, 0, 0))
    return pl.pallas_call(
        functools.partial(_s5_kernel, n_chunks=n_chunks, n_state=n_state),
        grid=(nb, bsz),
        in_specs=[pl.BlockSpec((None, seq, LANES), lambda m, b: (b, 0, m)),
                  const(base), const(w_ri), const(c_ri), const(a_re), const(a_im)],
        out_specs=pl.BlockSpec((None, seq, LANES), lambda m, b: (b, 0, m)),
        out_shape=jax.ShapeDtypeStruct((bsz, seq, width), F32),
        scratch_shapes=[pltpu.VMEM((n_chunks, kdim), BF16), pltpu.VMEM((n_chunks, 2 * n_state), F32),
                        pltpu.VMEM((n_chunks, 2 * n_state), F32), pltpu.VMEM((n_chunks, kdim), F32),
                        pltpu.VMEM((kdim, kdim), BF16), pltpu.VMEM((kdim, 2 * n_state), BF16),
                        pltpu.VMEM((2 * n_state, kdim), BF16)],
        compiler_params=_cparams(("parallel", "arbitrary")),
        name="s5_core",
    )(u3, base, w_ri, c_ri, a_re, a_im)


def _t5_bucket(rel):
    half = REL_BUCKETS // 2
    max_exact = half // 2
    ret = jnp.where(rel > 0, half, 0)
    n = jnp.abs(rel)
    large = max_exact + (jnp.log(jnp.maximum(n, 1).astype(jnp.float32) / max_exact)
                         / math.log(REL_MAX_DIST / max_exact) * (half - max_exact)).astype(jnp.int32)
    large = jnp.minimum(large, half - 1)
    return ret + jnp.where(n < max_exact, n, large)


def _bias_tiles(rel_bias, seq, tq):
    assert tq >= REL_MAX_DIST and tq % CHUNK == 0

    def bias_of(rel):
        bucket = _t5_bucket(rel)
        out = jnp.zeros((rel_bias.shape[1],) + rel.shape, F32)
        for b in range(REL_BUCKETS):
            out = jnp.where((bucket == b)[None], rel_bias[b].astype(F32)[:, None, None], out)
        return out

    far = bias_of(jnp.full((1, 1), -(seq - 1), jnp.int32))
    r = jnp.arange(tq)[:, None]
    c = jnp.arange(tq)[None, :]
    diag = (bias_of(c - r) - far) * LOG2E
    diag = jnp.where(((c // CHUNK) <= (r // CHUNK))[None], diag, NEG_INF)
    adj = (bias_of(c - r[:DIFF_UNIT] - tq) - far) * LOG2E
    return diag, adj


def _diff_kernel(lam_ref, q_ref, k_ref, v_ref, z_ref, bd_ref, ba_ref, w_ref, o_ref,
                 qs_ref, m_ref, acc_ref, s_ref, *, tq, out_scale):
    qi = pl.program_id(2)
    lam = lam_ref[0]
    ur = DIFF_UNIT
    n_units = tq // ur
    nt = (((1,), (1,)), ((), ()))
    lane = lax.broadcasted_iota(jnp.int32, (ur, LANES), 1)

    for u in range(n_units):
        qh = q_ref[u * ur:(u + 1) * ur, :].astype(F32)
        qs_ref[u, 0:ur, :] = jnp.where(lane < DIFF_HEAD_DIM, qh, 0.0).astype(BF16)
        qs_ref[u, ur:2 * ur, :] = jnp.where(lane >= DIFF_HEAD_DIM, qh, 0.0).astype(BF16)
    m_ref[...] = jnp.full(m_ref.shape, NEG_INF, F32)
    acc_ref[...] = jnp.zeros(acc_ref.shape, F32)

    def scores(u, k0, nk):
        return lax.dot_general(qs_ref[u], k_ref[pl.ds(k0, nk), :], nt, preferred_element_type=F32)

    def update(u, s, k0, nk):
        blocks = [s[:, j * LANES:(j + 1) * LANES] for j in range(nk // LANES)]
        m_old = m_ref[u]
        m_new = jnp.maximum(m_old, jnp.max(functools.reduce(jnp.maximum, blocks), axis=-1, keepdims=True))
        alpha = jnp.exp2(m_old - m_new)
        p = jnp.concatenate([jnp.exp2(blk - m_new) for blk in blocks], axis=1).astype(BF16)
        v_ext = jnp.concatenate([v_ref[pl.ds(k0, nk), :], jnp.ones((nk, LANES), BF16)], axis=1)
        pv = jnp.dot(p, v_ext, preferred_element_type=F32)
        acc_ref[u] = jnp.concatenate([alpha, alpha], axis=1) * acc_ref[u] + pv
        m_ref[u] = m_new

    def produce(u, j, slot):
        k0 = pl.multiple_of(j * tq, tq)
        s = scores(u, k0, tq)
        if u == 0:
            bias = ba_ref[...] * (j == qi - 1).astype(F32)
            s = s + jnp.concatenate([bias, bias], axis=0)
        s_ref[slot, u] = s

    def consume(u, j, slot):
        update(u, s_ref[slot, u], pl.multiple_of(j * tq, tq), tq)

    def step(j, slot):
        for u in range(n_units):
            produce(u, j + 1, 1 - slot)
            consume(u, j, slot)

    kd = pl.multiple_of(qi * tq, tq)
    s_diag = []
    for u in range(n_units):
        nk = (u + 1) * ur
        bias = bd_ref[u * ur:(u + 1) * ur, 0:nk]
        s_diag.append(scores(u, kd, nk) + jnp.concatenate([bias, bias], axis=0))
    for u in range(n_units):
        produce(u, 0, 0)
    for u in range(n_units):
        update(u, s_diag[u], kd, (u + 1) * ur)

    def pair(i, carry):
        step(2 * i, 0)
        step(2 * i + 1, 1)
        return carry

    lax.fori_loop(0, (qi - 1) // 2, pair, 0)

    def finalize(u):
        acc = acc_ref[u]
        o = acc[0:ur, 0:LANES] / acc[0:ur, LANES:] - lam * (acc[ur:, 0:LANES] / acc[ur:, LANES:])
        o = o * lax.rsqrt(jnp.mean(o * o, axis=-1, keepdims=True) + RMS_EPS) * w_ref[...]
        o = o * out_scale
        rows = pl.ds(u * ur, ur)
        o_ref[rows, :] = (o * z_ref[rows, :].astype(F32)).astype(o_ref.dtype)

    def drain(j, slot):
        for u in range(n_units):
            consume(u, j, slot)
            finalize(u)

    @pl.when(qi == 0)
    def _():
        for u in range(n_units):
            finalize(u)

    @pl.when(qi % 2 == 1)
    def _():
        drain(qi - 1, 0)

    @pl.when(jnp.logical_and(qi >= 2, qi % 2 == 0))
    def _():
        step(qi - 2, 0)
        drain(qi - 1, 1)


def _diff_attention(lam, q3, k3, v3, sz3, bias_diag, bias_adj, subln_w, lambda_init):
    bsz, seq, width = q3.shape
    tq = DIFF_TQ
    ur = DIFF_UNIT
    hd = 2 * DIFF_HEAD_DIM
    assert hd == LANES and ur >= REL_MAX_DIST
    qspec = pl.BlockSpec((None, tq, hd), lambda b, h, i: (b, i, h))
    kvspec = pl.BlockSpec((None, seq, hd), lambda b, h, i: (b, 0, h))
    return pl.pallas_call(
        functools.partial(_diff_kernel, tq=tq, out_scale=1.0 - lambda_init),
        grid=(bsz, width // hd, seq // tq),
        in_specs=[pl.BlockSpec(memory_space=pltpu.SMEM), qspec, kvspec, kvspec, qspec,
                  pl.BlockSpec((None, tq, tq), lambda b, h, i: (h, 0, 0)),
                  pl.BlockSpec((None, ur, tq), lambda b, h, i: (h, 0, 0)),
                  pl.BlockSpec((1, hd), lambda b, h, i: (0, 0))],
        out_specs=qspec,
        out_shape=jax.ShapeDtypeStruct((bsz, seq, width), BF16),
        scratch_shapes=[pltpu.VMEM((tq // ur, 2 * ur, hd), BF16), pltpu.VMEM((tq // ur, 2 * ur, LANES), F32),
                        pltpu.VMEM((tq // ur, 2 * ur, 2 * LANES), F32),
                        pltpu.VMEM((2, tq // ur, 2 * ur, tq), F32)],
        compiler_params=_cparams(("parallel", "parallel", "arbitrary")),
        name="diff_attn",
    )(lam, q3, k3, v3, sz3, bias_diag, bias_adj, subln_w)


def _merge_kernel(x_ref, y_ref, u_ref, szs_ref, yd_ref, mq_ref, szm_ref, g_ref, mkv_ref,
                  dsk_ref, wglu_ref, wbs_ref, wbd_ref, wbm_ref, wout_ref, lng_ref, lnb_ref, o_ref,
                  *, width, d_model, alpha):
    nt = (((1,), (1,)), ((), ()))
    tm = x_ref.shape[0]
    groups = [pl.ds(i * (tm // MERGE_GROUPS), tm // MERGE_GROUPS) for i in range(MERGE_GROUPS)]
    mm = lambda a, w_ref: jnp.dot(a, w_ref[...], preferred_element_type=F32)
    s_mem = [[lax.dot_general(mq_ref[rows, pl.ds(h * MEM_HEAD_DIM, MEM_HEAD_DIM)],
                              mkv_ref[:, pl.ds(h * MEM_HEAD_DIM, MEM_HEAD_DIM)], nt,
                              preferred_element_type=F32) for h in range(MEM_HEADS)] for rows in groups]
    proj_d = [mm(yd_ref[rows, :], wbd_ref) for rows in groups]
    glu = [mm(jax.nn.gelu(y_ref[rows, :] + dsk_ref[...] * u_ref[rows, :]).astype(BF16), wglu_ref)
           for rows in groups]
    y_mem = []
    for rows, s_heads in zip(groups, s_mem):
        heads = []
        for h, s in enumerate(s_heads):
            p = jnp.exp(s - jnp.max(s, axis=-1, keepdims=True))
            l = jnp.sum(p, axis=-1, keepdims=True)
            o = jnp.dot(p.astype(BF16), mkv_ref[:, pl.ds(width + h * MEM_HEAD_DIM, MEM_HEAD_DIM)],
                        preferred_element_type=F32)
            heads.append(o / l)
        y_mem.append((jnp.concatenate(heads, axis=-1) * szm_ref[rows, :].astype(F32)).astype(BF16))
    y_ssm = [(a[:, :width] * jax.nn.sigmoid(a[:, width:]) * szs_ref[rows, :].astype(F32)).astype(BF16)
             for rows, a in zip(groups, glu)]
    proj_s = [mm(a, wbs_ref) for a in y_ssm]
    proj_m = [mm(a, wbm_ref) for a in y_mem]
    merged = [(g_ref[rows, 0:d_model].astype(F32) * ps
               + g_ref[rows, d_model:2 * d_model].astype(F32) * pd
               + g_ref[rows, 2 * d_model:3 * d_model].astype(F32) * pm).astype(BF16)
              for rows, ps, pd, pm in zip(groups, proj_s, proj_d, proj_m)]
    outs = [mm(a, wout_ref) for a in merged]
    for rows, out in zip(groups, outs):
        hres = alpha * x_ref[rows, :] + out
        mu = jnp.mean(hres, axis=-1, keepdims=True)
        cen = hres - mu
        var = jnp.mean(cen * cen, axis=-1, keepdims=True)
        o_ref[rows, :] = cen * lax.rsqrt(var + LN_EPS) * lng_ref[...] + lnb_ref[...]


def _merge(x2, y2, u2, szs, yd, mq, szm, g, mkv, d_skip, w_glu, w_bs, w_bd, w_bm, w_out, ln_g, ln_b,
           seq, alpha):
    t, d_model = x2.shape
    width = y2.shape[1]
    n_mem = mkv.shape[1]
    tm = MERGE_TM
    per_b = seq // tm
    row = lambda n: pl.BlockSpec((tm, n), lambda i: (i, 0))
    full = lambda a: pl.BlockSpec(a.shape, lambda i: (0,) * a.ndim)
    return pl.pallas_call(
        functools.partial(_merge_kernel, width=width, d_model=d_model, alpha=alpha),
        grid=(t // tm,),
        in_specs=[row(d_model), row(width), row(width), row(width), row(width), row(width), row(width),
                  row(3 * d_model),
                  pl.BlockSpec((None, n_mem, 2 * width), lambda i: (i // per_b, 0, 0)),
                  full(d_skip), full(w_glu), full(w_bs), full(w_bd), full(w_bm), full(w_out),
                  full(ln_g), full(ln_b)],
        out_specs=row(d_model),
        out_shape=jax.ShapeDtypeStruct((t, d_model), F32),
        compiler_params=_cparams(("parallel",)),
        name="merge_out",
    )(x2, y2, u2, szs, yd, mq, szm, g, mkv, d_skip, w_glu, w_bs, w_bd, w_bm, w_out, ln_g, ln_b)


def kernel(x, mem, w_in, lam_re, lam_im, log_dt, b_re, b_im, c_re, c_im, d_skip, w_glu, lambda_q1, lambda_k1, lambda_q2, lambda_k2, subln_w, rel_bias, w_mem_kv, w_br_ssm, w_br_diff, w_br_mem, w_out, ln_g, ln_b):
    bsz, seq, d_model = x.shape
    depth = w_in.shape[0]
    width = d_model // 2
    n_mem = mem.shape[1]
    alpha = (2.0 * depth) ** 0.25
    bias_diag, bias_adj = _bias_tiles(rel_bias, seq, DIFF_TQ)
    mem2 = mem.reshape(bsz * n_mem, d_model)
    h = x.reshape(bsz * seq, d_model)
    for layer in range(depth):
        lambda_init = 0.8 - 0.6 * math.exp(-0.3 * layer)
        u, szs, dq, dk, dv, szd, mq, szm, g = _projection(h, w_in[layer].astype(BF16), width)
        mkv = _memkv(mem2, w_mem_kv[layer].astype(BF16)).reshape(bsz, n_mem, 2 * width)
        mats = _s5_matrices(lam_re[layer], lam_im[layer], log_dt[layer], b_re[layer], b_im[layer],
                            c_re[layer], c_im[layer])
        y = _s5_core(u.reshape(bsz, seq, width), *mats).reshape(bsz * seq, width)
        lam = (jnp.exp(jnp.sum(lambda_q1[layer].astype(F32) * lambda_k1[layer].astype(F32)))
               - jnp.exp(jnp.sum(lambda_q2[layer].astype(F32) * lambda_k2[layer].astype(F32)))
               + lambda_init).reshape(1)
        r3 = lambda a: a.reshape(bsz, seq, width)
        yd = _diff_attention(lam, r3(dq), r3(dk), r3(dv), r3(szd), bias_diag, bias_adj,
                             subln_w[layer].reshape(1, -1).astype(F32), lambda_init)
        h = _merge(h, y, u, szs, yd.reshape(bsz * seq, width), mq, szm, g, mkv,
                   d_skip[layer].reshape(1, -1), w_glu[layer].astype(BF16), w_br_ssm[layer].astype(BF16),
                   w_br_diff[layer].astype(BF16), w_br_mem[layer].astype(BF16), w_out[layer].astype(BF16),
                   ln_g[layer].reshape(1, -1), ln_b[layer].reshape(1, -1), seq, alpha)
    return h.reshape(bsz, seq, d_model)
```

```python
import functools
import math

import jax
import jax.numpy as jnp
from jax import lax
from jax.experimental import pallas as pl
from jax.experimental.pallas import tpu as pltpu

F32 = jnp.float32
BF16 = jnp.bfloat16

LANES = 128
VMEM_LIMIT_BYTES = 56 * 1024 * 1024

CHUNK = 64
SSM_GROUP = 16
SSM_STATE = 64
SSM_L = 16
DIFF_HEADS = 4
DIFF_HEAD_DIM = 64
MEM_HEADS = 4
MEM_HEAD_DIM = 128
REL_BUCKETS = 32
REL_MAX_DIST = 128
LN_EPS = 1e-5
RMS_EPS = 1e-5
NEG_INF = -1e30
LOG2E = math.log2(math.e)

PROJ_TM = 512
DIFF_TQ = 512
DIFF_UNIT = 128
MERGE_TM = 512
MERGE_GROUPS = 2


def _cparams(sem):
    return pltpu.CompilerParams(dimension_semantics=sem, vmem_limit_bytes=VMEM_LIMIT_BYTES)


def _silu(z):
    return z * jax.nn.sigmoid(z)


def _proj_kernel(x_ref, w_ref, u_ref, szs_ref, q_ref, k_ref, v_ref, szd_ref, mq_ref, szm_ref, g_ref,
                 *, width, mem_scale, diff_scale):
    xb = x_ref[...].astype(BF16)

    def mm(col, ncols):
        return jnp.dot(xb, w_ref[:, col:col + ncols], preferred_element_type=F32)

    w = width
    u_ref[...] = mm(0, w)
    szs_ref[...] = _silu(mm(w, w)).astype(BF16)
    q_ref[...] = (mm(2 * w, w) * diff_scale).astype(BF16)
    k_ref[...] = mm(3 * w, w).astype(BF16)
    v_ref[...] = mm(4 * w, w).astype(BF16)
    szd_ref[...] = _silu(mm(5 * w, w)).astype(BF16)
    mq_ref[...] = (mm(6 * w, w) * mem_scale).astype(BF16)
    szm_ref[...] = _silu(mm(7 * w, w)).astype(BF16)
    for j in range(6):
        g_ref[:, j * w:(j + 1) * w] = jax.nn.sigmoid(mm(8 * w + j * w, w)).astype(BF16)


def _projection(x2, w_bf, width):
    t, d = x2.shape
    d_in = w_bf.shape[1]
    tm = PROJ_TM
    row = lambda n: pl.BlockSpec((tm, n), lambda i: (i, 0))
    outs = [jax.ShapeDtypeStruct((t, width), F32)] + [jax.ShapeDtypeStruct((t, width), BF16)] * 7 \
        + [jax.ShapeDtypeStruct((t, 6 * width), BF16)]
    return pl.pallas_call(
        functools.partial(_proj_kernel, width=width, mem_scale=MEM_HEAD_DIM ** -0.5,
                          diff_scale=DIFF_HEAD_DIM ** -0.5 * LOG2E),
        grid=(t // tm,),
        in_specs=[row(d), pl.BlockSpec((d, d_in), lambda i: (0, 0), pipeline_mode=pl.Buffered(1))],
        out_specs=[row(width)] * 8 + [row(6 * width)],
        out_shape=outs,
        compiler_params=_cparams(("parallel",)),
        name="in_proj",
    )(x2, w_bf)


def _memkv_kernel(m_ref, w_ref, o_ref):
    o_ref[...] = jnp.dot(m_ref[...].astype(BF16), w_ref[...], preferred_element_type=F32).astype(BF16)


def _memkv(mem2, w_bf):
    r, d = mem2.shape
    n = w_bf.shape[1]
    tm = 512
    return pl.pallas_call(
        _memkv_kernel,
        grid=(r // tm,),
        in_specs=[pl.BlockSpec((tm, d), lambda i: (i, 0)), pl.BlockSpec((d, n), lambda i: (0, 0))],
        out_specs=pl.BlockSpec((tm, n), lambda i: (i, 0)),
        out_shape=jax.ShapeDtypeStruct((r, n), BF16),
        compiler_params=_cparams(("parallel",)),
        name="mem_kv",
    )(mem2, w_bf)


def _s5_matrices(lam_re, lam_im, log_dt, b_re, b_im, c_re, c_im):
    L = SSM_L
    g_n, p_n = lam_re.shape
    h_n = b_re.shape[-1]
    gl = LANES // h_n
    nb = g_n // gl
    sw = gl * p_n
    dt = jnp.exp(log_dt.astype(F32))[:, None]
    zr = lam_re.astype(F32) * dt
    zi = lam_im.astype(F32) * dt
    ks = jnp.arange(L + 1, dtype=F32)[:, None, None]
    mag = jnp.exp(zr[None] * ks)
    pr = mag * jnp.cos(zi[None] * ks)
    pi = mag * jnp.sin(zi[None] * ks)
    nr, ni = pr[1] - 1.0, pi[1]
    den = lam_re * lam_re + lam_im * lam_im
    fr = (nr * lam_re + ni * lam_im) / den
    fi = (ni * lam_re - nr * lam_im) / den
    bbr = fr[..., None] * b_re - fi[..., None] * b_im
    bbi = fr[..., None] * b_im + fi[..., None] * b_re

    rev = L - 1 - jnp.arange(L)
    pwl_r = pr[rev].reshape(L, g_n * p_n)
    pwl_i = pi[rev].reshape(L, g_n * p_n)
    bbl_r = bbr.transpose(2, 0, 1).reshape(h_n, g_n * p_n)
    bbl_i = bbi.transpose(2, 0, 1).reshape(h_n, g_n * p_n)
    w_r = pwl_r[:, None, :] * bbl_r[None] - pwl_i[:, None, :] * bbl_i[None]
    w_i = pwl_r[:, None, :] * bbl_i[None] + pwl_i[:, None, :] * bbl_r[None]
    blk = lambda w: w.reshape(L, h_n, nb, sw).transpose(2, 0, 1, 3)
    w_ri = jnp.concatenate([blk(w_r), blk(w_i)], axis=-1).reshape(nb, L * h_n, 2 * sw)

    cl_r = c_re.transpose(2, 0, 1).reshape(p_n, g_n * h_n)
    cl_i = c_im.transpose(2, 0, 1).reshape(p_n, g_n * h_n)
    pwx_r = jnp.repeat(pr.transpose(0, 2, 1), h_n, axis=-1)
    pwx_i = jnp.repeat(pi.transpose(0, 2, 1), h_n, axis=-1)
    cp_r = cl_r[None] * pwx_r - cl_i[None] * pwx_i
    cp_i = cl_r[None] * pwx_i + cl_i[None] * pwx_r
    oblk = lambda c: c.reshape(L, p_n, nb, LANES).transpose(2, 1, 0, 3).reshape(nb, p_n, L * LANES)
    c_ri = jnp.concatenate([oblk(cp_r[1:]), oblk(-cp_i[1:])], axis=1)

    bbx_r = jnp.repeat(bbr.transpose(2, 1, 0), h_n, axis=-1)
    bbx_i = jnp.repeat(bbi.transpose(2, 1, 0), h_n, axis=-1)
    kt = jnp.sum(cp_r[:L, None] * bbx_r[None] - cp_i[:L, None] * bbx_i[None], axis=2)
    base = kt.reshape(L, h_n, nb, LANES).transpose(2, 1, 0, 3).reshape(nb, h_n, L * LANES)
    a_re = pr[L].reshape(nb, 1, sw)
    a_im = pi[L].reshape(nb, 1, sw)
    return base, w_ri, c_ri, a_re, a_im


def _s5_expand(base_ref, w_ref, cf_ref, k_ref, b_ref, c_ref, *, n_state):
    L = SSM_L
    h_n = base_ref.shape[0]
    gl = LANES // h_n
    p_n = cf_ref.shape[0] // 2
    assert h_n & (h_n - 1) == 0 and p_n & (p_n - 1) == 0 and n_state & (n_state - 1) == 0
    lane_o = lax.broadcasted_iota(jnp.int32, (1, L * LANES), 1)
    lane_s = lax.broadcasted_iota(jnp.int32, (1, 2 * n_state), 1)
    group_o = jnp.right_shift(jnp.bitwise_and(lane_o, LANES - 1), h_n.bit_length() - 1)
    group_s = jnp.right_shift(jnp.bitwise_and(lane_s, n_state - 1), p_n.bit_length() - 1)
    for g in range(gl):
        base_g = jnp.where(group_o == g, base_ref[...], 0.0)
        for j in range(L):
            rows = pl.ds((j * gl + g) * h_n, h_n)
            if j > 0:
                k_ref[rows, 0:j * LANES] = jnp.zeros((h_n, j * LANES), BF16)
            k_ref[rows, j * LANES:L * LANES] = base_g[:, 0:(L - j) * LANES].astype(BF16)
            b_ref[rows, :] = jnp.where(group_s == g, w_ref[j * h_n:(j + 1) * h_n, :], 0.0).astype(BF16)
        for r in range(2):
            c_ref[pl.ds((r * gl + g) * p_n, p_n), :] = jnp.where(
                group_o == g, cf_ref[r * p_n:(r + 1) * p_n, :], 0.0).astype(BF16)


def _s5_kernel(u_ref, base_ref, w_ref, cf_ref, ar_ref, ai_ref, y_ref,
               u2_ref, v_ref, sp_ref, y2_ref, k_ref, b_ref, c_ref, *, n_chunks, n_state):
    L = SSM_L

    @pl.when(pl.program_id(1) == 0)
    def _():
        _s5_expand(base_ref, w_ref, cf_ref, k_ref, b_ref, c_ref, n_state=n_state)

    for i in range(L):
        u2_ref[:, i * LANES:(i + 1) * LANES] = u_ref[pl.ds(i, n_chunks, stride=L), :].astype(BF16)
    v_ref[...] = jnp.dot(u2_ref[...], b_ref[...], preferred_element_type=F32)
    ar = ar_ref[...]
    ai = ai_ref[...]

    wb = 2 * LANES
    for blk in range(L * LANES // wb):
        cols = slice(blk * wb, (blk + 1) * wb)
        rows = (blk + 1) * wb
        y2_ref[:, cols] = jnp.dot(u2_ref[:, 0:rows], k_ref[0:rows, cols], preferred_element_type=F32)
    sr = si = jnp.zeros((1, n_state), F32)
    for c in range(n_chunks):
        sp_ref[c:c + 1, 0:n_state] = sr
        sp_ref[c:c + 1, n_state:2 * n_state] = si
        vr = v_ref[c:c + 1, 0:n_state]
        vi = v_ref[c:c + 1, n_state:2 * n_state]
        sr, si = ar * sr - ai * si + vr, ar * si + ai * sr + vi
    y2_ref[...] += jnp.dot(sp_ref[...].astype(BF16), c_ref[...], preferred_element_type=F32)
    for i in range(L):
        y_ref[pl.ds(i, n_chunks, stride=L), :] = y2_ref[:, i * LANES:(i + 1) * LANES]


def _s5_core(u3, base, w_ri, c_ri, a_re, a_im):
    bsz, seq, width = u3.shape
    nb = width // LANES
    n_chunks = seq // SSM_L
    n_state = a_re.shape[-1]
    kdim = SSM_L * LANES
    const = lambda a: pl.BlockSpec((None,) + a.shape[1:], lambda m, b: (m, 0, 0))
    return pl.pallas_call(
        functools.partial(_s5_kernel, n_chunks=n_chunks, n_state=n_state),
        grid=(nb, bsz),
        in_specs=[pl.BlockSpec((None, seq, LANES), lambda m, b: (b, 0, m)),
                  const(base), const(w_ri), const(c_ri), const(a_re), const(a_im)],
        out_specs=pl.BlockSpec((None, seq, LANES), lambda m, b: (b, 0, m)),
        out_shape=jax.ShapeDtypeStruct((bsz, seq, width), F32),
        scratch_shapes=[pltpu.VMEM((n_chunks, kdim), BF16), pltpu.VMEM((n_chunks, 2 * n_state), F32),
                        pltpu.VMEM((n_chunks, 2 * n_state), F32), pltpu.VMEM((n_chunks, kdim), F32),
                        pltpu.VMEM((kdim, kdim), BF16), pltpu.VMEM((kdim, 2 * n_state), BF16),
                        pltpu.VMEM((2 * n_state, kdim), BF16)],
        compiler_params=_cparams(("parallel", "arbitrary")),
        name="s5_core",
    )(u3, base, w_ri, c_ri, a_re, a_im)


def _t5_bucket(rel):
    half = REL_BUCKETS // 2
    max_exact = half // 2
    ret = jnp.where(rel > 0, half, 0)
    n = jnp.abs(rel)
    large = max_exact + (jnp.log(jnp.maximum(n, 1).astype(jnp.float32) / max_exact)
                         / math.log(REL_MAX_DIST / max_exact) * (half - max_exact)).astype(jnp.int32)
    large = jnp.minimum(large, half - 1)
    return ret + jnp.where(n < max_exact, n, large)


def _bias_tiles(rel_bias, seq, tq):
    assert tq >= REL_MAX_DIST and tq % CHUNK == 0 and DIFF_UNIT <= tq
    n = 2 * tq
    heads = rel_bias.shape[1]
    k = jnp.arange(n)
    rel_d = jnp.where(k < tq, k, k - n)
    rel = jnp.concatenate([rel_d, rel_d - tq, jnp.full((1,), -(seq - 1), jnp.int32)])
    onehot = (_t5_bucket(rel)[:, None] == jnp.arange(REL_BUCKETS)[None, :]).astype(F32)
    table = jnp.dot(onehot, rel_bias.astype(F32), precision=lax.Precision.HIGHEST)
    table = ((table[:2 * n] - table[2 * n:]) * LOG2E).T.reshape(heads, 2, 1, n)

    def expand(w_ref, d_ref, a_ref):
        rolled = pltpu.roll(jnp.broadcast_to(w_ref[0], (tq, n)), 0, 1, stride=1, stride_axis=0)
        r = lax.broadcasted_iota(jnp.int32, (tq, tq), 0)
        c = lax.broadcasted_iota(jnp.int32, (tq, tq), 1)
        allowed = jnp.right_shift(c, CHUNK.bit_length() - 1) <= jnp.right_shift(r, CHUNK.bit_length() - 1)
        d_ref[...] = jnp.where(allowed, rolled[:, 0:tq], NEG_INF)
        a_ref[...] = pltpu.roll(jnp.broadcast_to(w_ref[1], (DIFF_UNIT, n)), 0, 1,
                                stride=1, stride_axis=0)[:, 0:tq]

    assert CHUNK & (CHUNK - 1) == 0
    return pl.pallas_call(
        expand,
        grid=(heads,),
        in_specs=[pl.BlockSpec((None, 2, 1, n), lambda h: (h, 0, 0, 0))],
        out_specs=[pl.BlockSpec((None, tq, tq), lambda h: (h, 0, 0)),
                   pl.BlockSpec((None, DIFF_UNIT, tq), lambda h: (h, 0, 0))],
        out_shape=[jax.ShapeDtypeStruct((heads, tq, tq), F32),
                   jax.ShapeDtypeStruct((heads, DIFF_UNIT, tq), F32)],
        compiler_params=_cparams(("parallel",)),
        name="bias_tiles",
    )(table)


def _diff_kernel(lam_ref, q_ref, k_ref, v_ref, z_ref, bd_ref, ba_ref, w_ref, o_ref,
                 qs_ref, m_ref, acc_ref, s_ref, *, tq, out_scale):
    qi = pl.program_id(2)
    lam = lam_ref[0]
    ur = DIFF_UNIT
    n_units = tq // ur
    nt = (((1,), (1,)), ((), ()))
    lane = lax.broadcasted_iota(jnp.int32, (ur, LANES), 1)

    for u in range(n_units):
        qh = q_ref[u * ur:(u + 1) * ur, :].astype(F32)
        qs_ref[u, 0:ur, :] = jnp.where(lane < DIFF_HEAD_DIM, qh, 0.0).astype(BF16)
        qs_ref[u, ur:2 * ur, :] = jnp.where(lane >= DIFF_HEAD_DIM, qh, 0.0).astype(BF16)
    m_ref[...] = jnp.full(m_ref.shape, NEG_INF, F32)
    acc_ref[...] = jnp.zeros(acc_ref.shape, F32)

    def scores(u, k0, nk):
        return lax.dot_general(qs_ref[u], k_ref[pl.ds(k0, nk), :], nt, preferred_element_type=F32)

    def update(u, s, k0, nk):
        blocks = [s[:, j * LANES:(j + 1) * LANES] for j in range(nk // LANES)]
        m_old = m_ref[u]
        m_new = jnp.maximum(m_old, jnp.max(functools.reduce(jnp.maximum, blocks), axis=-1, keepdims=True))
        alpha = jnp.exp2(m_old - m_new)
        p = jnp.concatenate([jnp.exp2(blk - m_new) for blk in blocks], axis=1).astype(BF16)
        v_ext = jnp.concatenate([v_ref[pl.ds(k0, nk), :], jnp.ones((nk, LANES), BF16)], axis=1)
        pv = jnp.dot(p, v_ext, preferred_element_type=F32)
        acc_ref[u] = jnp.concatenate([alpha, alpha], axis=1) * acc_ref[u] + pv
        m_ref[u] = m_new

    def produce(u, j, slot):
        k0 = pl.multiple_of(j * tq, tq)
        s = scores(u, k0, tq)
        if u == 0:
            bias = ba_ref[...] * (j == qi - 1).astype(F32)
            s = s + jnp.concatenate([bias, bias], axis=0)
        s_ref[slot, u] = s

    def consume(u, j, slot):
        update(u, s_ref[slot, u], pl.multiple_of(j * tq, tq), tq)

    def step(j, slot):
        for u in range(n_units):
            produce(u, j + 1, 1 - slot)
            consume(u, j, slot)

    kd = pl.multiple_of(qi * tq, tq)
    s_diag = []
    for u in range(n_units):
        nk = (u + 1) * ur
        bias = bd_ref[u * ur:(u + 1) * ur, 0:nk]
        s_diag.append(scores(u, kd, nk) + jnp.concatenate([bias, bias], axis=0))
    for u in range(n_units):
        produce(u, 0, 0)
    for u in range(n_units):
        update(u, s_diag[u], kd, (u + 1) * ur)

    def pair(i, carry):
        step(2 * i, 0)
        step(2 * i + 1, 1)
        return carry

    lax.fori_loop(0, (qi - 1) // 2, pair, 0)

    def finalize(u):
        acc = acc_ref[u]
        o = acc[0:ur, 0:LANES] / acc[0:ur, LANES:] - lam * (acc[ur:, 0:LANES] / acc[ur:, LANES:])
        o = o * lax.rsqrt(jnp.mean(o * o, axis=-1, keepdims=True) + RMS_EPS) * w_ref[...]
        o = o * out_scale
        rows = pl.ds(u * ur, ur)
        o_ref[rows, :] = (o * z_ref[rows, :].astype(F32)).astype(o_ref.dtype)

    def drain(j, slot):
        for u in range(n_units):
            consume(u, j, slot)
            finalize(u)

    @pl.when(qi == 0)
    def _():
        for u in range(n_units):
            finalize(u)

    @pl.when(qi % 2 == 1)
    def _():
        drain(qi - 1, 0)

    @pl.when(jnp.logical_and(qi >= 2, qi % 2 == 0))
    def _():
        step(qi - 2, 0)
        drain(qi - 1, 1)


def _diff_attention(lam, q3, k3, v3, sz3, bias_diag, bias_adj, subln_w, lambda_init):
    bsz, seq, width = q3.shape
    tq = DIFF_TQ
    ur = DIFF_UNIT
    hd = 2 * DIFF_HEAD_DIM
    assert hd == LANES and ur >= REL_MAX_DIST
    qspec = pl.BlockSpec((None, tq, hd), lambda b, h, i: (b, i, h))
    kvspec = pl.BlockSpec((None, seq, hd), lambda b, h, i: (b, 0, h))
    return pl.pallas_call(
        functools.partial(_diff_kernel, tq=tq, out_scale=1.0 - lambda_init),
        grid=(bsz, width // hd, seq // tq),
        in_specs=[pl.BlockSpec(memory_space=pltpu.SMEM), qspec, kvspec, kvspec, qspec,
                  pl.BlockSpec((None, tq, tq), lambda b, h, i: (h, 0, 0)),
                  pl.BlockSpec((None, ur, tq), lambda b, h, i: (h, 0, 0)),
                  pl.BlockSpec((1, hd), lambda b, h, i: (0, 0))],
        out_specs=qspec,
        out_shape=jax.ShapeDtypeStruct((bsz, seq, width), BF16),
        scratch_shapes=[pltpu.VMEM((tq // ur, 2 * ur, hd), BF16), pltpu.VMEM((tq // ur, 2 * ur, LANES), F32),
                        pltpu.VMEM((tq // ur, 2 * ur, 2 * LANES), F32),
                        pltpu.VMEM((2, tq // ur, 2 * ur, tq), F32)],
        compiler_params=_cparams(("parallel", "parallel", "arbitrary")),
        name="diff_attn",
    )(lam, q3, k3, v3, sz3, bias_diag, bias_adj, subln_w)


def _merge_kernel(x_ref, y_ref, u_ref, szs_ref, yd_ref, mq_ref, szm_ref, g_ref, mkv_ref,
                  dsk_ref, wglu_ref, wbs_ref, wbd_ref, wbm_ref, wout_ref, lng_ref, lnb_ref, o_ref,
                  *, width, d_model, alpha):
    nt = (((1,), (1,)), ((), ()))
    tm = x_ref.shape[0]
    groups = [pl.ds(i * (tm // MERGE_GROUPS), tm // MERGE_GROUPS) for i in range(MERGE_GROUPS)]
    mm = lambda a, w_ref: jnp.dot(a, w_ref[...], preferred_element_type=F32)
    s_mem = [[lax.dot_general(mq_ref[rows, pl.ds(h * MEM_HEAD_DIM, MEM_HEAD_DIM)],
                              mkv_ref[:, pl.ds(h * MEM_HEAD_DIM, MEM_HEAD_DIM)], nt,
                              preferred_element_type=F32) for h in range(MEM_HEADS)] for rows in groups]
    proj_d = [mm(yd_ref[rows, :], wbd_ref) for rows in groups]
    glu = [mm(jax.nn.gelu(y_ref[rows, :] + dsk_ref[...] * u_ref[rows, :]).astype(BF16), wglu_ref)
           for rows in groups]
    y_mem = []
    for rows, s_heads in zip(groups, s_mem):
        heads = []
        for h, s in enumerate(s_heads):
            p = jnp.exp(s - jnp.max(s, axis=-1, keepdims=True))
            l = jnp.sum(p, axis=-1, keepdims=True)
            o = jnp.dot(p.astype(BF16), mkv_ref[:, pl.ds(width + h * MEM_HEAD_DIM, MEM_HEAD_DIM)],
                        preferred_element_type=F32)
            heads.append(o / l)
        y_mem.append((jnp.concatenate(heads, axis=-1) * szm_ref[rows, :].astype(F32)).astype(BF16))
    y_ssm = [(a[:, :width] * jax.nn.sigmoid(a[:, width:]) * szs_ref[rows, :].astype(F32)).astype(BF16)
             for rows, a in zip(groups, glu)]
    proj_s = [mm(a, wbs_ref) for a in y_ssm]
    proj_m = [mm(a, wbm_ref) for a in y_mem]
    merged = [(g_ref[rows, 0:d_model].astype(F32) * ps
               + g_ref[rows, d_model:2 * d_model].astype(F32) * pd
               + g_ref[rows, 2 * d_model:3 * d_model].astype(F32) * pm).astype(BF16)
              for rows, ps, pd, pm in zip(groups, proj_s, proj_d, proj_m)]
    outs = [mm(a, wout_ref) for a in merged]
    for rows, out in zip(groups, outs):
        hres = alpha * x_ref[rows, :] + out
        mu = jnp.mean(hres, axis=-1, keepdims=True)
        cen = hres - mu
        var = jnp.mean(cen * cen, axis=-1, keepdims=True)
        o_ref[rows, :] = cen * lax.rsqrt(var + LN_EPS) * lng_ref[...] + lnb_ref[...]


def _merge(x2, y2, u2, szs, yd, mq, szm, g, mkv, d_skip, w_glu, w_bs, w_bd, w_bm, w_out, ln_g, ln_b,
           seq, alpha):
    t, d_model = x2.shape
    width = y2.shape[1]
    n_mem = mkv.shape[1]
    tm = MERGE_TM
    per_b = seq // tm
    row = lambda n: pl.BlockSpec((tm, n), lambda i: (i, 0))
    full = lambda a: pl.BlockSpec(a.shape, lambda i: (0,) * a.ndim)
    return pl.pallas_call(
        functools.partial(_merge_kernel, width=width, d_model=d_model, alpha=alpha),
        grid=(t // tm,),
        in_specs=[row(d_model), row(width), row(width), row(width), row(width), row(width), row(width),
                  row(3 * d_model),
                  pl.BlockSpec((None, n_mem, 2 * width), lambda i: (i // per_b, 0, 0)),
                  full(d_skip), full(w_glu), full(w_bs), full(w_bd), full(w_bm), full(w_out),
                  full(ln_g), full(ln_b)],
        out_specs=row(d_model),
        out_shape=jax.ShapeDtypeStruct((t, d_model), F32),
        compiler_params=_cparams(("parallel",)),
        name="merge_out",
    )(x2, y2, u2, szs, yd, mq, szm, g, mkv, d_skip, w_glu, w_bs, w_bd, w_bm, w_out, ln_g, ln_b)


def kernel(x, mem, w_in, lam_re, lam_im, log_dt, b_re, b_im, c_re, c_im, d_skip, w_glu, lambda_q1, lambda_k1, lambda_q2, lambda_k2, subln_w, rel_bias, w_mem_kv, w_br_ssm, w_br_diff, w_br_mem, w_out, ln_g, ln_b):
    bsz, seq, d_model = x.shape
    depth = w_in.shape[0]
    width = d_model // 2
    n_mem = mem.shape[1]
    alpha = (2.0 * depth) ** 0.25
    bias_diag, bias_adj = _bias_tiles(rel_bias, seq, DIFF_TQ)
    mem2 = mem.reshape(bsz * n_mem, d_model)
    h = x.reshape(bsz * seq, d_model)
    for layer in range(depth):
        lambda_init = 0.8 - 0.6 * math.exp(-0.3 * layer)
        u, szs, dq, dk, dv, szd, mq, szm, g = _projection(h, w_in[layer].astype(BF16), width)
        mkv = _memkv(mem2, w_mem_kv[layer].astype(BF16)).reshape(bsz, n_mem, 2 * width)
        mats = _s5_matrices(lam_re[layer], lam_im[layer], log_dt[layer], b_re[layer], b_im[layer],
                            c_re[layer], c_im[layer])
        y = _s5_core(u.reshape(bsz, seq, width), *mats).reshape(bsz * seq, width)
        lam = (jnp.exp(jnp.sum(lambda_q1[layer].astype(F32) * lambda_k1[layer].astype(F32)))
               - jnp.exp(jnp.sum(lambda_q2[layer].astype(F32) * lambda_k2[layer].astype(F32)))
               + lambda_init).reshape(1)
        r3 = lambda a: a.reshape(bsz, seq, width)
        yd = _diff_attention(lam, r3(dq), r3(dk), r3(dv), r3(szd), bias_diag, bias_adj,
                             subln_w[layer].reshape(1, -1).astype(F32), lambda_init)
        h = _merge(h, y, u, szs, yd.reshape(bsz * seq, width), mq, szm, g, mkv,
                   d_skip[layer].reshape(1, -1), w_glu[layer].astype(BF16), w_br_ssm[layer].astype(BF16),
                   w_br_diff[layer].astype(BF16), w_br_mem[layer].astype(BF16), w_out[layer].astype(BF16),
                   ln_g[layer].reshape(1, -1), ln_b[layer].reshape(1, -1), seq, alpha)
    return h.reshape(bsz, seq, d_model)
```

```python
import functools
import math

import jax
import jax.numpy as jnp
from jax import lax
from jax.experimental import pallas as pl
from jax.experimental.pallas import tpu as pltpu

F32 = jnp.float32
BF16 = jnp.bfloat16

LANES = 128
VMEM_LIMIT_BYTES = 56 * 1024 * 1024

CHUNK = 64
SSM_GROUP = 16
SSM_STATE = 64
SSM_L = 16
DIFF_HEADS = 4
DIFF_HEAD_DIM = 64
MEM_HEADS = 4
MEM_HEAD_DIM = 128
REL_BUCKETS = 32
REL_MAX_DIST = 128
LN_EPS = 1e-5
RMS_EPS = 1e-5
NEG_INF = -1e30
LOG2E = math.log2(math.e)

PROJ_TM = 512
DIFF_TQ = 512
DIFF_UNIT = 128
DIFF_HEADS_PER_STEP = 2
MERGE_TM = 512
MERGE_GROUPS = 2


def _cparams(sem):
    return pltpu.CompilerParams(dimension_semantics=sem, vmem_limit_bytes=VMEM_LIMIT_BYTES)


def _silu(z):
    return z * jax.nn.sigmoid(z)


def _proj_kernel(x_ref, w_ref, u_ref, szs_ref, q_ref, k_ref, v_ref, szd_ref, mq_ref, szm_ref, g_ref,
                 *, width, mem_scale, diff_scale):
    xb = x_ref[...].astype(BF16)

    def mm(col, ncols):
        return jnp.dot(xb, w_ref[:, col:col + ncols], preferred_element_type=F32)

    w = width
    u_ref[...] = mm(0, w)
    szs_ref[...] = _silu(mm(w, w)).astype(BF16)
    q_ref[...] = (mm(2 * w, w) * diff_scale).astype(BF16)
    k_ref[...] = mm(3 * w, w).astype(BF16)
    v_ref[...] = mm(4 * w, w).astype(BF16)
    szd_ref[...] = _silu(mm(5 * w, w)).astype(BF16)
    mq_ref[...] = (mm(6 * w, w) * mem_scale).astype(BF16)
    szm_ref[...] = _silu(mm(7 * w, w)).astype(BF16)
    for j in range(6):
        g_ref[:, j * w:(j + 1) * w] = jax.nn.sigmoid(mm(8 * w + j * w, w)).astype(BF16)


def _projection(x2, w_bf, width):
    t, d = x2.shape
    d_in = w_bf.shape[1]
    tm = PROJ_TM
    row = lambda n: pl.BlockSpec((tm, n), lambda i: (i, 0))
    outs = [jax.ShapeDtypeStruct((t, width), F32)] + [jax.ShapeDtypeStruct((t, width), BF16)] * 7 \
        + [jax.ShapeDtypeStruct((t, 6 * width), BF16)]
    return pl.pallas_call(
        functools.partial(_proj_kernel, width=width, mem_scale=MEM_HEAD_DIM ** -0.5,
                          diff_scale=DIFF_HEAD_DIM ** -0.5 * LOG2E),
        grid=(t // tm,),
        in_specs=[row(d), pl.BlockSpec((d, d_in), lambda i: (0, 0), pipeline_mode=pl.Buffered(1))],
        out_specs=[row(width)] * 8 + [row(6 * width)],
        out_shape=outs,
        compiler_params=_cparams(("parallel",)),
        name="in_proj",
    )(x2, w_bf)


def _memkv_kernel(m_ref, w_ref, o_ref):
    o_ref[...] = jnp.dot(m_ref[...].astype(BF16), w_ref[...], preferred_element_type=F32).astype(BF16)


def _memkv(mem2, w_bf):
    r, d = mem2.shape
    n = w_bf.shape[1]
    tm = 512
    return pl.pallas_call(
        _memkv_kernel,
        grid=(r // tm,),
        in_specs=[pl.BlockSpec((tm, d), lambda i: (i, 0)), pl.BlockSpec((d, n), lambda i: (0, 0))],
        out_specs=pl.BlockSpec((tm, n), lambda i: (i, 0)),
        out_shape=jax.ShapeDtypeStruct((r, n), BF16),
        compiler_params=_cparams(("parallel",)),
        name="mem_kv",
    )(mem2, w_bf)


def _s5_matrices(lam_re, lam_im, log_dt, b_re, b_im, c_re, c_im):
    L = SSM_L
    g_n, p_n = lam_re.shape
    h_n = b_re.shape[-1]
    gl = LANES // h_n
    nb = g_n // gl
    sw = gl * p_n
    dt = jnp.exp(log_dt.astype(F32))[:, None]
    zr = lam_re.astype(F32) * dt
    zi = lam_im.astype(F32) * dt
    ks = jnp.arange(L + 1, dtype=F32)[:, None, None]
    mag = jnp.exp(zr[None] * ks)
    pr = mag * jnp.cos(zi[None] * ks)
    pi = mag * jnp.sin(zi[None] * ks)
    nr, ni = pr[1] - 1.0, pi[1]
    den = lam_re * lam_re + lam_im * lam_im
    fr = (nr * lam_re + ni * lam_im) / den
    fi = (ni * lam_re - nr * lam_im) / den
    bbr = fr[..., None] * b_re - fi[..., None] * b_im
    bbi = fr[..., None] * b_im + fi[..., None] * b_re

    rev = L - 1 - jnp.arange(L)
    pwl_r = pr[rev].reshape(L, g_n * p_n)
    pwl_i = pi[rev].reshape(L, g_n * p_n)
    bbl_r = bbr.transpose(2, 0, 1).reshape(h_n, g_n * p_n)
    bbl_i = bbi.transpose(2, 0, 1).reshape(h_n, g_n * p_n)
    w_r = pwl_r[:, None, :] * bbl_r[None] - pwl_i[:, None, :] * bbl_i[None]
    w_i = pwl_r[:, None, :] * bbl_i[None] + pwl_i[:, None, :] * bbl_r[None]
    blk = lambda w: w.reshape(L, h_n, nb, sw).transpose(2, 0, 1, 3)
    w_ri = jnp.concatenate([blk(w_r), blk(w_i)], axis=-1).reshape(nb, L * h_n, 2 * sw)

    cl_r = c_re.transpose(2, 0, 1).reshape(p_n, g_n * h_n)
    cl_i = c_im.transpose(2, 0, 1).reshape(p_n, g_n * h_n)
    pwx_r = jnp.repeat(pr.transpose(0, 2, 1), h_n, axis=-1)
    pwx_i = jnp.repeat(pi.transpose(0, 2, 1), h_n, axis=-1)
    cp_r = cl_r[None] * pwx_r - cl_i[None] * pwx_i
    cp_i = cl_r[None] * pwx_i + cl_i[None] * pwx_r
    oblk = lambda c: c.reshape(L, p_n, nb, LANES).transpose(2, 1, 0, 3).reshape(nb, p_n, L * LANES)
    c_ri = jnp.concatenate([oblk(cp_r[1:]), oblk(-cp_i[1:])], axis=1)

    bbx_r = jnp.repeat(bbr.transpose(2, 1, 0), h_n, axis=-1)
    bbx_i = jnp.repeat(bbi.transpose(2, 1, 0), h_n, axis=-1)
    kt = jnp.sum(cp_r[:L, None] * bbx_r[None] - cp_i[:L, None] * bbx_i[None], axis=2)
    base = kt.reshape(L, h_n, nb, LANES).transpose(2, 1, 0, 3).reshape(nb, h_n, L * LANES)
    a_re = pr[L].reshape(nb, 1, sw)
    a_im = pi[L].reshape(nb, 1, sw)
    return base, w_ri, c_ri, a_re, a_im


def _s5_expand(base_ref, w_ref, cf_ref, k_ref, b_ref, c_ref, *, n_state):
    L = SSM_L
    h_n = base_ref.shape[0]
    gl = LANES // h_n
    p_n = cf_ref.shape[0] // 2
    assert h_n & (h_n - 1) == 0 and p_n & (p_n - 1) == 0 and n_state & (n_state - 1) == 0
    lane_o = lax.broadcasted_iota(jnp.int32, (1, L * LANES), 1)
    lane_s = lax.broadcasted_iota(jnp.int32, (1, 2 * n_state), 1)
    group_o = jnp.right_shift(jnp.bitwise_and(lane_o, LANES - 1), h_n.bit_length() - 1)
    group_s = jnp.right_shift(jnp.bitwise_and(lane_s, n_state - 1), p_n.bit_length() - 1)
    for g in range(gl):
        base_g = jnp.where(group_o == g, base_ref[...], 0.0)
        for j in range(L):
            rows = pl.ds((j * gl + g) * h_n, h_n)
            if j > 0:
                k_ref[rows, 0:j * LANES] = jnp.zeros((h_n, j * LANES), BF16)
            k_ref[rows, j * LANES:L * LANES] = base_g[:, 0:(L - j) * LANES].astype(BF16)
            b_ref[rows, :] = jnp.where(group_s == g, w_ref[j * h_n:(j + 1) * h_n, :], 0.0).astype(BF16)
        for r in range(2):
            c_ref[pl.ds((r * gl + g) * p_n, p_n), :] = jnp.where(
                group_o == g, cf_ref[r * p_n:(r + 1) * p_n, :], 0.0).astype(BF16)


def _s5_kernel(u_ref, base_ref, w_ref, cf_ref, ar_ref, ai_ref, y_ref,
               u2_ref, v_ref, sp_ref, y2_ref, k_ref, b_ref, c_ref, *, n_chunks, n_state):
    L = SSM_L

    @pl.when(pl.program_id(1) == 0)
    def _():
        _s5_expand(base_ref, w_ref, cf_ref, k_ref, b_ref, c_ref, n_state=n_state)

    for i in range(L):
        u2_ref[:, i * LANES:(i + 1) * LANES] = u_ref[pl.ds(i, n_chunks, stride=L), :].astype(BF16)
    v_ref[...] = jnp.dot(u2_ref[...], b_ref[...], preferred_element_type=F32)
    ar = ar_ref[...]
    ai = ai_ref[...]

    wb = 2 * LANES
    for blk in range(L * LANES // wb):
        cols = slice(blk * wb, (blk + 1) * wb)
        rows = (blk + 1) * wb
        y2_ref[:, cols] = jnp.dot(u2_ref[:, 0:rows], k_ref[0:rows, cols], preferred_element_type=F32)
    sr = si = jnp.zeros((1, n_state), F32)
    for c in range(n_chunks):
        sp_ref[c:c + 1, 0:n_state] = sr
        sp_ref[c:c + 1, n_state:2 * n_state] = si
        vr = v_ref[c:c + 1, 0:n_state]
        vi = v_ref[c:c + 1, n_state:2 * n_state]
        sr, si = ar * sr - ai * si + vr, ar * si + ai * sr + vi
    y2_ref[...] += jnp.dot(sp_ref[...].astype(BF16), c_ref[...], preferred_element_type=F32)
    for i in range(L):
        y_ref[pl.ds(i, n_chunks, stride=L), :] = y2_ref[:, i * LANES:(i + 1) * LANES]


def _s5_core(u3, base, w_ri, c_ri, a_re, a_im):
    bsz, seq, width = u3.shape
    nb = width // LANES
    n_chunks = seq // SSM_L
    n_state = a_re.shape[-1]
    kdim = SSM_L * LANES
    const = lambda a: pl.BlockSpec((None,) + a.shape[1:], lambda m, b: (m, 0, 0))
    return pl.pallas_call(
        functools.partial(_s5_kernel, n_chunks=n_chunks, n_state=n_state),
        grid=(nb, bsz),
        in_specs=[pl.BlockSpec((None, seq, LANES), lambda m, b: (b, 0, m)),
                  const(base), const(w_ri), const(c_ri), const(a_re), const(a_im)],
        out_specs=pl.BlockSpec((None, seq, LANES), lambda m, b: (b, 0, m)),
        out_shape=jax.ShapeDtypeStruct((bsz, seq, width), F32),
        scratch_shapes=[pltpu.VMEM((n_chunks, kdim), BF16), pltpu.VMEM((n_chunks, 2 * n_state), F32),
                        pltpu.VMEM((n_chunks, 2 * n_state), F32), pltpu.VMEM((n_chunks, kdim), F32),
                        pltpu.VMEM((kdim, kdim), BF16), pltpu.VMEM((kdim, 2 * n_state), BF16),
                        pltpu.VMEM((2 * n_state, kdim), BF16)],
        compiler_params=_cparams(("parallel", "arbitrary")),
        name="s5_core",
    )(u3, base, w_ri, c_ri, a_re, a_im)


def _t5_bucket(rel):
    half = REL_BUCKETS // 2
    max_exact = half // 2
    ret = jnp.where(rel > 0, half, 0)
    n = jnp.abs(rel)
    large = max_exact + (jnp.log(jnp.maximum(n, 1).astype(jnp.float32) / max_exact)
                         / math.log(REL_MAX_DIST / max_exact) * (half - max_exact)).astype(jnp.int32)
    large = jnp.minimum(large, half - 1)
    return ret + jnp.where(n < max_exact, n, large)


def _bias_tiles(rel_bias, seq, tq):
    assert tq >= REL_MAX_DIST and tq % CHUNK == 0 and DIFF_UNIT <= tq
    n = 2 * tq
    heads = rel_bias.shape[1]
    k = jnp.arange(n)
    rel_d = jnp.where(k < tq, k, k - n)
    rel = jnp.concatenate([rel_d, rel_d - tq, jnp.full((1,), -(seq - 1), jnp.int32)])
    onehot = (_t5_bucket(rel)[:, None] == jnp.arange(REL_BUCKETS)[None, :]).astype(F32)
    table = jnp.dot(onehot, rel_bias.astype(F32), precision=lax.Precision.HIGHEST)
    table = ((table[:2 * n] - table[2 * n:]) * LOG2E).T.reshape(heads, 2, 1, n)

    def expand(w_ref, d_ref, a_ref):
        rolled = pltpu.roll(jnp.broadcast_to(w_ref[0], (tq, n)), 0, 1, stride=1, stride_axis=0)
        r = lax.broadcasted_iota(jnp.int32, (tq, tq), 0)
        c = lax.broadcasted_iota(jnp.int32, (tq, tq), 1)
        allowed = jnp.right_shift(c, CHUNK.bit_length() - 1) <= jnp.right_shift(r, CHUNK.bit_length() - 1)
        d_ref[...] = jnp.where(allowed, rolled[:, 0:tq], NEG_INF)
        a_ref[...] = pltpu.roll(jnp.broadcast_to(w_ref[1], (DIFF_UNIT, n)), 0, 1,
                                stride=1, stride_axis=0)[:, 0:tq]

    assert CHUNK & (CHUNK - 1) == 0
    return pl.pallas_call(
        expand,
        grid=(heads,),
        in_specs=[pl.BlockSpec((None, 2, 1, n), lambda h: (h, 0, 0, 0))],
        out_specs=[pl.BlockSpec((None, tq, tq), lambda h: (h, 0, 0)),
                   pl.BlockSpec((None, DIFF_UNIT, tq), lambda h: (h, 0, 0))],
        out_shape=[jax.ShapeDtypeStruct((heads, tq, tq), F32),
                   jax.ShapeDtypeStruct((heads, DIFF_UNIT, tq), F32)],
        compiler_params=_cparams(("parallel",)),
        name="bias_tiles",
    )(table)


def _diff_kernel(lam_ref, q_ref, k_ref, v_ref, z_ref, bd_ref, ba_ref, w_ref, o_ref,
                 qs_ref, m_ref, acc_ref, s_ref, *, tq, out_scale):
    qi = pl.program_id(2)
    lam = lam_ref[0]
    ur = DIFF_UNIT
    hd = 2 * DIFF_HEAD_DIM
    n_heads = q_ref.shape[1] // hd
    n_units = (tq // ur) * n_heads
    unit_row = lambda u: u // n_heads
    unit_cols = lambda u: pl.ds((u % n_heads) * hd, hd)
    nt = (((1,), (1,)), ((), ()))
    lane = lax.broadcasted_iota(jnp.int32, (ur, LANES), 1)

    for u in range(n_units):
        qh = q_ref[pl.ds(unit_row(u) * ur, ur), unit_cols(u)].astype(F32)
        qs_ref[u, 0:ur, :] = jnp.where(lane < DIFF_HEAD_DIM, qh, 0.0).astype(BF16)
        qs_ref[u, ur:2 * ur, :] = jnp.where(lane >= DIFF_HEAD_DIM, qh, 0.0).astype(BF16)
    m_ref[...] = jnp.full(m_ref.shape, NEG_INF, F32)
    acc_ref[...] = jnp.zeros(acc_ref.shape, F32)

    def scores(u, k0, nk):
        return lax.dot_general(qs_ref[u], k_ref[pl.ds(k0, nk), unit_cols(u)], nt,
                               preferred_element_type=F32)

    def update(u, s, k0, nk):
        blocks = [s[:, j * LANES:(j + 1) * LANES] for j in range(nk // LANES)]
        m_old = m_ref[u]
        m_new = jnp.maximum(m_old, jnp.max(functools.reduce(jnp.maximum, blocks), axis=-1, keepdims=True))
        alpha = jnp.exp2(m_old - m_new)
        p = jnp.concatenate([jnp.exp2(blk - m_new) for blk in blocks], axis=1).astype(BF16)
        v_ext = jnp.concatenate([v_ref[pl.ds(k0, nk), unit_cols(u)], jnp.ones((nk, LANES), BF16)], axis=1)
        pv = jnp.dot(p, v_ext, preferred_element_type=F32)
        acc_ref[u] = jnp.concatenate([alpha, alpha], axis=1) * acc_ref[u] + pv
        m_ref[u] = m_new

    def produce(u, j, slot):
        k0 = pl.multiple_of(j * tq, tq)
        s = scores(u, k0, tq)
        if unit_row(u) == 0:
            bias = ba_ref[u % n_heads] * (j == qi - 1).astype(F32)
            s = s + jnp.concatenate([bias, bias], axis=0)
        s_ref[slot, u] = s

    def consume(u, j, slot):
        update(u, s_ref[slot, u], pl.multiple_of(j * tq, tq), tq)

    def step(j, slot):
        for u in range(n_units):
            produce(u, j + 1, 1 - slot)
            consume(u, j, slot)

    kd = pl.multiple_of(qi * tq, tq)
    s_diag = []
    for u in range(n_units):
        r = unit_row(u)
        nk = (r + 1) * ur
        bias = bd_ref[u % n_heads, r * ur:(r + 1) * ur, 0:nk]
        s_diag.append(scores(u, kd, nk) + jnp.concatenate([bias, bias], axis=0))
    for u in range(n_units):
        produce(u, 0, 0)
    for u in range(n_units):
        update(u, s_diag[u], kd, (unit_row(u) + 1) * ur)

    def pair(i, carry):
        step(2 * i, 0)
        step(2 * i + 1, 1)
        return carry

    lax.fori_loop(0, (qi - 1) // 2, pair, 0)

    def finalize(u):
        acc = acc_ref[u]
        o = acc[0:ur, 0:LANES] / acc[0:ur, LANES:] - lam * (acc[ur:, 0:LANES] / acc[ur:, LANES:])
        o = o * lax.rsqrt(jnp.mean(o * o, axis=-1, keepdims=True) + RMS_EPS) * w_ref[...]
        o = o * out_scale
        rows = pl.ds(unit_row(u) * ur, ur)
        o_ref[rows, unit_cols(u)] = (o * z_ref[rows, unit_cols(u)].astype(F32)).astype(o_ref.dtype)

    def drain(j, slot):
        for u in range(n_units):
            consume(u, j, slot)
            finalize(u)

    @pl.when(qi == 0)
    def _():
        for u in range(n_units):
            finalize(u)

    @pl.when(qi % 2 == 1)
    def _():
        drain(qi - 1, 0)

    @pl.when(jnp.logical_and(qi >= 2, qi % 2 == 0))
    def _():
        step(qi - 2, 0)
        drain(qi - 1, 1)


def _diff_attention(lam, q3, k3, v3, sz3, bias_diag, bias_adj, subln_w, lambda_init):
    bsz, seq, width = q3.shape
    tq = DIFF_TQ
    ur = DIFF_UNIT
    hd = 2 * DIFF_HEAD_DIM
    assert hd == LANES and ur >= REL_MAX_DIST
    hp = DIFF_HEADS_PER_STEP
    n_units = (tq // ur) * hp
    qspec = pl.BlockSpec((None, tq, hp * hd), lambda b, h, i: (b, i, h))
    kvspec = pl.BlockSpec((None, seq, hp * hd), lambda b, h, i: (b, 0, h))
    return pl.pallas_call(
        functools.partial(_diff_kernel, tq=tq, out_scale=1.0 - lambda_init),
        grid=(bsz, width // (hp * hd), seq // tq),
        in_specs=[pl.BlockSpec(memory_space=pltpu.SMEM), qspec, kvspec, kvspec, qspec,
                  pl.BlockSpec((hp, tq, tq), lambda b, h, i: (h, 0, 0), pipeline_mode=pl.Buffered(1)),
                  pl.BlockSpec((hp, ur, tq), lambda b, h, i: (h, 0, 0), pipeline_mode=pl.Buffered(1)),
                  pl.BlockSpec((1, hd), lambda b, h, i: (0, 0))],
        out_specs=qspec,
        out_shape=jax.ShapeDtypeStruct((bsz, seq, width), BF16),
        scratch_shapes=[pltpu.VMEM((n_units, 2 * ur, hd), BF16), pltpu.VMEM((n_units, 2 * ur, LANES), F32),
                        pltpu.VMEM((n_units, 2 * ur, 2 * LANES), F32),
                        pltpu.VMEM((2, n_units, 2 * ur, tq), F32)],
        compiler_params=_cparams(("parallel", "parallel", "arbitrary")),
        name="diff_attn",
    )(lam, q3, k3, v3, sz3, bias_diag, bias_adj, subln_w)


def _merge_kernel(x_ref, y_ref, u_ref, szs_ref, yd_ref, mq_ref, szm_ref, g_ref, mkv_ref,
                  dsk_ref, wglu_ref, wbs_ref, wbd_ref, wbm_ref, wout_ref, lng_ref, lnb_ref, o_ref,
                  *, width, d_model, alpha):
    nt = (((1,), (1,)), ((), ()))
    tm = x_ref.shape[0]
    groups = [pl.ds(i * (tm // MERGE_GROUPS), tm // MERGE_GROUPS) for i in range(MERGE_GROUPS)]
    mm = lambda a, w_ref: jnp.dot(a, w_ref[...], preferred_element_type=F32)
    s_mem = [[lax.dot_general(mq_ref[rows, pl.ds(h * MEM_HEAD_DIM, MEM_HEAD_DIM)],
                              mkv_ref[:, pl.ds(h * MEM_HEAD_DIM, MEM_HEAD_DIM)], nt,
                              preferred_element_type=F32) for h in range(MEM_HEADS)] for rows in groups]
    proj_d = [mm(yd_ref[rows, :], wbd_ref) for rows in groups]
    glu = [mm(jax.nn.gelu(y_ref[rows, :] + dsk_ref[...] * u_ref[rows, :]).astype(BF16), wglu_ref)
           for rows in groups]
    y_mem = []
    for rows, s_heads in zip(groups, s_mem):
        heads = []
        for h, s in enumerate(s_heads):
            p = jnp.exp(s - jnp.max(s, axis=-1, keepdims=True))
            l = jnp.sum(p, axis=-1, keepdims=True)
            o = jnp.dot(p.astype(BF16), mkv_ref[:, pl.ds(width + h * MEM_HEAD_DIM, MEM_HEAD_DIM)],
                        preferred_element_type=F32)
            heads.append(o / l)
        y_mem.append((jnp.concatenate(heads, axis=-1) * szm_ref[rows, :].astype(F32)).astype(BF16))
    y_ssm = [(a[:, :width] * jax.nn.sigmoid(a[:, width:]) * szs_ref[rows, :].astype(F32)).astype(BF16)
             for rows, a in zip(groups, glu)]
    proj_s = [mm(a, wbs_ref) for a in y_ssm]
    proj_m = [mm(a, wbm_ref) for a in y_mem]
    merged = [(g_ref[rows, 0:d_model].astype(F32) * ps
               + g_ref[rows, d_model:2 * d_model].astype(F32) * pd
               + g_ref[rows, 2 * d_model:3 * d_model].astype(F32) * pm).astype(BF16)
              for rows, ps, pd, pm in zip(groups, proj_s, proj_d, proj_m)]
    outs = [mm(a, wout_ref) for a in merged]
    for rows, out in zip(groups, outs):
        hres = alpha * x_ref[rows, :] + out
        mu = jnp.mean(hres, axis=-1, keepdims=True)
        cen = hres - mu
        var = jnp.mean(cen * cen, axis=-1, keepdims=True)
        o_ref[rows, :] = cen * lax.rsqrt(var + LN_EPS) * lng_ref[...] + lnb_ref[...]


def _merge(x2, y2, u2, szs, yd, mq, szm, g, mkv, d_skip, w_glu, w_bs, w_bd, w_bm, w_out, ln_g, ln_b,
           seq, alpha):
    t, d_model = x2.shape
    width = y2.shape[1]
    n_mem = mkv.shape[1]
    tm = MERGE_TM
    per_b = seq // tm
    row = lambda n: pl.BlockSpec((tm, n), lambda i: (i, 0))
    full = lambda a: pl.BlockSpec(a.shape, lambda i: (0,) * a.ndim)
    return pl.pallas_call(
        functools.partial(_merge_kernel, width=width, d_model=d_model, alpha=alpha),
        grid=(t // tm,),
        in_specs=[row(d_model), row(width), row(width), row(width), row(width), row(width), row(width),
                  row(3 * d_model),
                  pl.BlockSpec((None, n_mem, 2 * width), lambda i: (i // per_b, 0, 0)),
                  full(d_skip), full(w_glu), full(w_bs), full(w_bd), full(w_bm), full(w_out),
                  full(ln_g), full(ln_b)],
        out_specs=row(d_model),
        out_shape=jax.ShapeDtypeStruct((t, d_model), F32),
        compiler_params=_cparams(("parallel",)),
        name="merge_out",
    )(x2, y2, u2, szs, yd, mq, szm, g, mkv, d_skip, w_glu, w_bs, w_bd, w_bm, w_out, ln_g, ln_b)


def kernel(x, mem, w_in, lam_re, lam_im, log_dt, b_re, b_im, c_re, c_im, d_skip, w_glu, lambda_q1, lambda_k1, lambda_q2, lambda_k2, subln_w, rel_bias, w_mem_kv, w_br_ssm, w_br_diff, w_br_mem, w_out, ln_g, ln_b):
    bsz, seq, d_model = x.shape
    depth = w_in.shape[0]
    width = d_model // 2
    n_mem = mem.shape[1]
    alpha = (2.0 * depth) ** 0.25
    bias_diag, bias_adj = _bias_tiles(rel_bias, seq, DIFF_TQ)
    mem2 = mem.reshape(bsz * n_mem, d_model)
    h = x.reshape(bsz * seq, d_model)
    for layer in range(depth):
        lambda_init = 0.8 - 0.6 * math.exp(-0.3 * layer)
        u, szs, dq, dk, dv, szd, mq, szm, g = _projection(h, w_in[layer].astype(BF16), width)
        mkv = _memkv(mem2, w_mem_kv[layer].astype(BF16)).reshape(bsz, n_mem, 2 * width)
        mats = _s5_matrices(lam_re[layer], lam_im[layer], log_dt[layer], b_re[layer], b_im[layer],
                            c_re[layer], c_im[layer])
        y = _s5_core(u.reshape(bsz, seq, width), *mats).reshape(bsz * seq, width)
        lam = (jnp.exp(jnp.sum(lambda_q1[layer].astype(F32) * lambda_k1[layer].astype(F32)))
               - jnp.exp(jnp.sum(lambda_q2[layer].astype(F32) * lambda_k2[layer].astype(F32)))
               + lambda_init).reshape(1)
        r3 = lambda a: a.reshape(bsz, seq, width)
        yd = _diff_attention(lam, r3(dq), r3(dk), r3(dv), r3(szd), bias_diag, bias_adj,
                             subln_w[layer].reshape(1, -1).astype(F32), lambda_init)
        h = _merge(h, y, u, szs, yd.reshape(bsz * seq, width), mq, szm, g, mkv,
                   d_skip[layer].reshape(1, -1), w_glu[layer].astype(BF16), w_br_ssm[layer].astype(BF16),
                   w_br_diff[layer].astype(BF16), w_br_mem[layer].astype(BF16), w_out[layer].astype(BF16),
                   ln_g[layer].reshape(1, -1), ln_b[layer].reshape(1, -1), seq, alpha)
    return h.reshape(bsz, seq, d_model)
```

```python
import functools
import math

import jax
import jax.numpy as jnp
from jax import lax
from jax.experimental import pallas as pl
from jax.experimental.pallas import tpu as pltpu

F32 = jnp.float32
BF16 = jnp.bfloat16

LANES = 128
VMEM_LIMIT_BYTES = 56 * 1024 * 1024

CHUNK = 64
SSM_GROUP = 16
SSM_STATE = 64
SSM_L = 16
DIFF_HEADS = 4
DIFF_HEAD_DIM = 64
MEM_HEADS = 4
MEM_HEAD_DIM = 128
REL_BUCKETS = 32
REL_MAX_DIST = 128
LN_EPS = 1e-5
RMS_EPS = 1e-5
NEG_INF = -1e30
LOG2E = math.log2(math.e)

PROJ_TM = 512
DIFF_TQ = 512
DIFF_UNIT = 128
DIFF_HEADS_PER_STEP = 2
MERGE_TM = 512
MERGE_GROUPS = 2


def _cparams(sem):
    return pltpu.CompilerParams(dimension_semantics=sem, vmem_limit_bytes=VMEM_LIMIT_BYTES)


def _silu(z):
    return z * jax.nn.sigmoid(z)


def _proj_kernel(x_ref, w_ref, u_ref, szs_ref, q_ref, k_ref, v_ref, szd_ref, mq_ref, szm_ref, g_ref,
                 *, width, mem_scale, diff_scale):
    xb = x_ref[...].astype(BF16)

    def mm(col, ncols):
        return jnp.dot(xb, w_ref[:, col:col + ncols], preferred_element_type=F32)

    w = width
    a = mm(0, 2 * w)
    u_ref[...] = a[:, :w]
    szs_ref[...] = _silu(a[:, w:]).astype(BF16)
    a = mm(4 * w, 2 * w)
    v_ref[...] = a[:, :w].astype(BF16)
    szd_ref[...] = _silu(a[:, w:]).astype(BF16)
    a = mm(6 * w, 2 * w)
    mq_ref[...] = (a[:, :w] * mem_scale).astype(BF16)
    szm_ref[...] = _silu(a[:, w:]).astype(BF16)
    for j in range(3):
        g_ref[:, 2 * j * w:2 * (j + 1) * w] = jax.nn.sigmoid(mm(8 * w + 2 * j * w, 2 * w)).astype(BF16)
    a = mm(2 * w, 2 * w)
    q_ref[...] = (a[:, :w] * diff_scale).astype(BF16)
    k_ref[...] = a[:, w:].astype(BF16)


def _projection(x2, w_bf, width):
    t, d = x2.shape
    d_in = w_bf.shape[1]
    tm = PROJ_TM
    row = lambda n: pl.BlockSpec((tm, n), lambda i: (i, 0))
    outs = [jax.ShapeDtypeStruct((t, width), F32)] + [jax.ShapeDtypeStruct((t, width), BF16)] * 7 \
        + [jax.ShapeDtypeStruct((t, 6 * width), BF16)]
    return pl.pallas_call(
        functools.partial(_proj_kernel, width=width, mem_scale=MEM_HEAD_DIM ** -0.5,
                          diff_scale=DIFF_HEAD_DIM ** -0.5 * LOG2E),
        grid=(t // tm,),
        in_specs=[row(d), pl.BlockSpec((d, d_in), lambda i: (0, 0), pipeline_mode=pl.Buffered(1))],
        out_specs=[row(width)] * 8 + [row(6 * width)],
        out_shape=outs,
        compiler_params=_cparams(("parallel",)),
        name="in_proj",
    )(x2, w_bf)


def _memkv_kernel(m_ref, w_ref, o_ref):
    o_ref[...] = jnp.dot(m_ref[...].astype(BF16), w_ref[...], preferred_element_type=F32).astype(BF16)


def _memkv(mem2, w_bf):
    r, d = mem2.shape
    n = w_bf.shape[1]
    tm = 512
    return pl.pallas_call(
        _memkv_kernel,
        grid=(r // tm,),
        in_specs=[pl.BlockSpec((tm, d), lambda i: (i, 0)), pl.BlockSpec((d, n), lambda i: (0, 0))],
        out_specs=pl.BlockSpec((tm, n), lambda i: (i, 0)),
        out_shape=jax.ShapeDtypeStruct((r, n), BF16),
        compiler_params=_cparams(("parallel",)),
        name="mem_kv",
    )(mem2, w_bf)


def _s5_matrices(lam_re, lam_im, log_dt, b_re, b_im, c_re, c_im):
    L = SSM_L
    g_n, p_n = lam_re.shape
    h_n = b_re.shape[-1]
    gl = LANES // h_n
    nb = g_n // gl
    sw = gl * p_n
    dt = jnp.exp(log_dt.astype(F32))[:, None]
    zr = lam_re.astype(F32) * dt
    zi = lam_im.astype(F32) * dt
    ks = jnp.arange(L + 1, dtype=F32)[:, None, None]
    mag = jnp.exp(zr[None] * ks)
    pr = mag * jnp.cos(zi[None] * ks)
    pi = mag * jnp.sin(zi[None] * ks)
    nr, ni = pr[1] - 1.0, pi[1]
    den = lam_re * lam_re + lam_im * lam_im
    fr = (nr * lam_re + ni * lam_im) / den
    fi = (ni * lam_re - nr * lam_im) / den
    bbr = fr[..., None] * b_re - fi[..., None] * b_im
    bbi = fr[..., None] * b_im + fi[..., None] * b_re

    rev = L - 1 - jnp.arange(L)
    pwl_r = pr[rev].reshape(L, g_n * p_n)
    pwl_i = pi[rev].reshape(L, g_n * p_n)
    bbl_r = bbr.transpose(2, 0, 1).reshape(h_n, g_n * p_n)
    bbl_i = bbi.transpose(2, 0, 1).reshape(h_n, g_n * p_n)
    w_r = pwl_r[:, None, :] * bbl_r[None] - pwl_i[:, None, :] * bbl_i[None]
    w_i = pwl_r[:, None, :] * bbl_i[None] + pwl_i[:, None, :] * bbl_r[None]
    blk = lambda w: w.reshape(L, h_n, nb, sw).transpose(2, 0, 1, 3)
    w_ri = jnp.concatenate([blk(w_r), blk(w_i)], axis=-1).reshape(nb, L * h_n, 2 * sw)

    cl_r = c_re.transpose(2, 0, 1).reshape(p_n, g_n * h_n)
    cl_i = c_im.transpose(2, 0, 1).reshape(p_n, g_n * h_n)
    pwx_r = jnp.repeat(pr.transpose(0, 2, 1), h_n, axis=-1)
    pwx_i = jnp.repeat(pi.transpose(0, 2, 1), h_n, axis=-1)
    cp_r = cl_r[None] * pwx_r - cl_i[None] * pwx_i
    cp_i = cl_r[None] * pwx_i + cl_i[None] * pwx_r
    oblk = lambda c: c.reshape(L, p_n, nb, LANES).transpose(2, 1, 0, 3).reshape(nb, p_n, L * LANES)
    c_ri = jnp.concatenate([oblk(cp_r[1:]), oblk(-cp_i[1:])], axis=1)

    bbx_r = jnp.repeat(bbr.transpose(2, 1, 0), h_n, axis=-1)
    bbx_i = jnp.repeat(bbi.transpose(2, 1, 0), h_n, axis=-1)
    kt = jnp.sum(cp_r[:L, None] * bbx_r[None] - cp_i[:L, None] * bbx_i[None], axis=2)
    base = kt.reshape(L, h_n, nb, LANES).transpose(2, 1, 0, 3).reshape(nb, h_n, L * LANES)
    a_re = pr[L].reshape(nb, 1, sw)
    a_im = pi[L].reshape(nb, 1, sw)
    return base, w_ri, c_ri, a_re, a_im


def _s5_expand(base_ref, w_ref, cf_ref, k_ref, b_ref, c_ref, *, n_state):
    L = SSM_L
    h_n = base_ref.shape[0]
    gl = LANES // h_n
    p_n = cf_ref.shape[0] // 2
    assert h_n & (h_n - 1) == 0 and p_n & (p_n - 1) == 0 and n_state & (n_state - 1) == 0
    lane_o = lax.broadcasted_iota(jnp.int32, (1, L * LANES), 1)
    lane_s = lax.broadcasted_iota(jnp.int32, (1, 2 * n_state), 1)
    group_o = jnp.right_shift(jnp.bitwise_and(lane_o, LANES - 1), h_n.bit_length() - 1)
    group_s = jnp.right_shift(jnp.bitwise_and(lane_s, n_state - 1), p_n.bit_length() - 1)
    for g in range(gl):
        base_g = jnp.where(group_o == g, base_ref[...], 0.0)
        for j in range(L):
            rows = pl.ds((j * gl + g) * h_n, h_n)
            if j > 0:
                k_ref[rows, 0:j * LANES] = jnp.zeros((h_n, j * LANES), BF16)
            k_ref[rows, j * LANES:L * LANES] = base_g[:, 0:(L - j) * LANES].astype(BF16)
            b_ref[rows, :] = jnp.where(group_s == g, w_ref[j * h_n:(j + 1) * h_n, :], 0.0).astype(BF16)
        for r in range(2):
            c_ref[pl.ds((r * gl + g) * p_n, p_n), :] = jnp.where(
                group_o == g, cf_ref[r * p_n:(r + 1) * p_n, :], 0.0).astype(BF16)


def _s5_kernel(u_ref, base_ref, w_ref, cf_ref, ar_ref, ai_ref, y_ref,
               u2_ref, v_ref, sp_ref, y2_ref, k_ref, b_ref, c_ref, *, n_chunks, n_state):
    L = SSM_L

    @pl.when(pl.program_id(1) == 0)
    def _():
        _s5_expand(base_ref, w_ref, cf_ref, k_ref, b_ref, c_ref, n_state=n_state)

    for i in range(L):
        u2_ref[:, i * LANES:(i + 1) * LANES] = u_ref[pl.ds(i, n_chunks, stride=L), :].astype(BF16)
    v_ref[...] = jnp.dot(u2_ref[...], b_ref[...], preferred_element_type=F32)
    ar = ar_ref[...]
    ai = ai_ref[...]

    wb = 2 * LANES
    for blk in range(L * LANES // wb):
        cols = slice(blk * wb, (blk + 1) * wb)
        rows = (blk + 1) * wb
        y2_ref[:, cols] = jnp.dot(u2_ref[:, 0:rows], k_ref[0:rows, cols], preferred_element_type=F32)
    sr = si = jnp.zeros((1, n_state), F32)
    for c in range(n_chunks):
        sp_ref[c:c + 1, 0:n_state] = sr
        sp_ref[c:c + 1, n_state:2 * n_state] = si
        vr = v_ref[c:c + 1, 0:n_state]
        vi = v_ref[c:c + 1, n_state:2 * n_state]
        sr, si = ar * sr - ai * si + vr, ar * si + ai * sr + vi
    y2_ref[...] += jnp.dot(sp_ref[...].astype(BF16), c_ref[...], preferred_element_type=F32)
    for i in range(L):
        y_ref[pl.ds(i, n_chunks, stride=L), :] = y2_ref[:, i * LANES:(i + 1) * LANES]


def _s5_core(u3, base, w_ri, c_ri, a_re, a_im):
    bsz, seq, width = u3.shape
    nb = width // LANES
    n_chunks = seq // SSM_L
    n_state = a_re.shape[-1]
    kdim = SSM_L * LANES
    const = lambda a: pl.BlockSpec((None,) + a.shape[1:], lambda m, b: (m, 0, 0))
    return pl.pallas_call(
        functools.partial(_s5_kernel, n_chunks=n_chunks, n_state=n_state),
        grid=(nb, bsz),
        in_specs=[pl.BlockSpec((None, seq, LANES), lambda m, b: (b, 0, m)),
                  const(base), const(w_ri), const(c_ri), const(a_re), const(a_im)],
        out_specs=pl.BlockSpec((None, seq, LANES), lambda m, b: (b, 0, m)),
        out_shape=jax.ShapeDtypeStruct((bsz, seq, width), F32),
        scratch_shapes=[pltpu.VMEM((n_chunks, kdim), BF16), pltpu.VMEM((n_chunks, 2 * n_state), F32),
                        pltpu.VMEM((n_chunks, 2 * n_state), F32), pltpu.VMEM((n_chunks, kdim), F32),
                        pltpu.VMEM((kdim, kdim), BF16), pltpu.VMEM((kdim, 2 * n_state), BF16),
                        pltpu.VMEM((2 * n_state, kdim), BF16)],
        compiler_params=_cparams(("parallel", "arbitrary")),
        name="s5_core",
    )(u3, base, w_ri, c_ri, a_re, a_im)


def _t5_bucket(rel):
    half = REL_BUCKETS // 2
    max_exact = half // 2
    ret = jnp.where(rel > 0, half, 0)
    n = jnp.abs(rel)
    large = max_exact + (jnp.log(jnp.maximum(n, 1).astype(jnp.float32) / max_exact)
                         / math.log(REL_MAX_DIST / max_exact) * (half - max_exact)).astype(jnp.int32)
    large = jnp.minimum(large, half - 1)
    return ret + jnp.where(n < max_exact, n, large)


def _bias_tiles(rel_bias, seq, tq):
    assert tq >= REL_MAX_DIST and tq % CHUNK == 0 and DIFF_UNIT <= tq
    n = 2 * tq
    heads = rel_bias.shape[1]
    k = jnp.arange(n)
    rel_d = jnp.where(k < tq, k, k - n)
    rel = jnp.concatenate([rel_d, rel_d - tq, jnp.full((1,), -(seq - 1), jnp.int32)])
    onehot = (_t5_bucket(rel)[:, None] == jnp.arange(REL_BUCKETS)[None, :]).astype(F32)
    table = jnp.dot(onehot, rel_bias.astype(F32), precision=lax.Precision.HIGHEST)
    table = ((table[:2 * n] - table[2 * n:]) * LOG2E).T.reshape(heads, 2, 1, n)

    def expand(w_ref, d_ref, a_ref):
        rolled = pltpu.roll(jnp.broadcast_to(w_ref[0], (tq, n)), 0, 1, stride=1, stride_axis=0)
        r = lax.broadcasted_iota(jnp.int32, (tq, tq), 0)
        c = lax.broadcasted_iota(jnp.int32, (tq, tq), 1)
        allowed = jnp.right_shift(c, CHUNK.bit_length() - 1) <= jnp.right_shift(r, CHUNK.bit_length() - 1)
        d_ref[...] = jnp.where(allowed, rolled[:, 0:tq], NEG_INF)
        a_ref[...] = pltpu.roll(jnp.broadcast_to(w_ref[1], (DIFF_UNIT, n)), 0, 1,
                                stride=1, stride_axis=0)[:, 0:tq]

    assert CHUNK & (CHUNK - 1) == 0
    return pl.pallas_call(
        expand,
        grid=(heads,),
        in_specs=[pl.BlockSpec((None, 2, 1, n), lambda h: (h, 0, 0, 0))],
        out_specs=[pl.BlockSpec((None, tq, tq), lambda h: (h, 0, 0)),
                   pl.BlockSpec((None, DIFF_UNIT, tq), lambda h: (h, 0, 0))],
        out_shape=[jax.ShapeDtypeStruct((heads, tq, tq), F32),
                   jax.ShapeDtypeStruct((heads, DIFF_UNIT, tq), F32)],
        compiler_params=_cparams(("parallel",)),
        name="bias_tiles",
    )(table)


def _diff_kernel(lam_ref, q_ref, k_ref, v_ref, z_ref, bd_ref, ba_ref, w_ref, o_ref,
                 qs_ref, m_ref, acc_ref, s_ref, *, tq, out_scale):
    qi = pl.program_id(2)
    lam = lam_ref[0]
    ur = DIFF_UNIT
    hd = 2 * DIFF_HEAD_DIM
    n_heads = q_ref.shape[1] // hd
    n_units = (tq // ur) * n_heads
    unit_row = lambda u: u // n_heads
    unit_cols = lambda u: pl.ds((u % n_heads) * hd, hd)
    nt = (((1,), (1,)), ((), ()))
    lane = lax.broadcasted_iota(jnp.int32, (ur, LANES), 1)

    for u in range(n_units):
        qh = q_ref[pl.ds(unit_row(u) * ur, ur), unit_cols(u)].astype(F32)
        qs_ref[u, 0:ur, :] = jnp.where(lane < DIFF_HEAD_DIM, qh, 0.0).astype(BF16)
        qs_ref[u, ur:2 * ur, :] = jnp.where(lane >= DIFF_HEAD_DIM, qh, 0.0).astype(BF16)
    m_ref[...] = jnp.full(m_ref.shape, NEG_INF, F32)
    acc_ref[...] = jnp.zeros(acc_ref.shape, F32)

    def scores(u, k0, nk):
        return lax.dot_general(qs_ref[u], k_ref[pl.ds(k0, nk), unit_cols(u)], nt,
                               preferred_element_type=F32)

    def update(u, s, k0, nk):
        blocks = [s[:, j * LANES:(j + 1) * LANES] for j in range(nk // LANES)]
        m_old = m_ref[u]
        m_new = jnp.maximum(m_old, jnp.max(functools.reduce(jnp.maximum, blocks), axis=-1, keepdims=True))
        alpha = jnp.exp2(m_old - m_new)
        p = jnp.concatenate([jnp.exp2(blk - m_new) for blk in blocks], axis=1).astype(BF16)
        v_ext = jnp.concatenate([v_ref[pl.ds(k0, nk), unit_cols(u)], jnp.ones((nk, LANES), BF16)], axis=1)
        pv = jnp.dot(p, v_ext, preferred_element_type=F32)
        acc_ref[u] = jnp.concatenate([alpha, alpha], axis=1) * acc_ref[u] + pv
        m_ref[u] = m_new

    def produce(u, j, slot):
        k0 = pl.multiple_of(j * tq, tq)
        s = scores(u, k0, tq)
        if unit_row(u) == 0:
            bias = ba_ref[u % n_heads] * (j == qi - 1).astype(F32)
            s = s + jnp.concatenate([bias, bias], axis=0)
        s_ref[slot, u] = s

    def consume(u, j, slot):
        update(u, s_ref[slot, u], pl.multiple_of(j * tq, tq), tq)

    def step(j, slot):
        for u in range(n_units):
            consume(u, j, slot)
            produce(u, j + 1, 1 - slot)

    kd = pl.multiple_of(qi * tq, tq)
    s_diag = []
    for u in range(n_units):
        r = unit_row(u)
        nk = (r + 1) * ur
        bias = bd_ref[u % n_heads, r * ur:(r + 1) * ur, 0:nk]
        s_diag.append(scores(u, kd, nk) + jnp.concatenate([bias, bias], axis=0))
    for u in range(n_units):
        produce(u, 0, 0)
    for u in range(n_units):
        update(u, s_diag[u], kd, (unit_row(u) + 1) * ur)

    def pair(i, carry):
        step(2 * i, 0)
        step(2 * i + 1, 1)
        return carry

    lax.fori_loop(0, (qi - 1) // 2, pair, 0)

    def finalize(u):
        acc = acc_ref[u]
        o = acc[0:ur, 0:LANES] / acc[0:ur, LANES:] - lam * (acc[ur:, 0:LANES] / acc[ur:, LANES:])
        o = o * lax.rsqrt(jnp.mean(o * o, axis=-1, keepdims=True) + RMS_EPS) * w_ref[...]
        o = o * out_scale
        rows = pl.ds(unit_row(u) * ur, ur)
        o_ref[rows, unit_cols(u)] = (o * z_ref[rows, unit_cols(u)].astype(F32)).astype(o_ref.dtype)

    def drain(j, slot):
        for u in range(n_units):
            consume(u, j, slot)
            finalize(u)

    @pl.when(qi == 0)
    def _():
        for u in range(n_units):
            finalize(u)

    @pl.when(qi % 2 == 1)
    def _():
        drain(qi - 1, 0)

    @pl.when(jnp.logical_and(qi >= 2, qi % 2 == 0))
    def _():
        step(qi - 2, 0)
        drain(qi - 1, 1)


def _diff_attention(lam, q3, k3, v3, sz3, bias_diag, bias_adj, subln_w, lambda_init):
    bsz, seq, width = q3.shape
    tq = DIFF_TQ
    ur = DIFF_UNIT
    hd = 2 * DIFF_HEAD_DIM
    assert hd == LANES and ur >= REL_MAX_DIST
    hp = DIFF_HEADS_PER_STEP
    n_units = (tq // ur) * hp
    qspec = pl.BlockSpec((None, tq, hp * hd), lambda b, h, i: (b, i, h))
    kvspec = pl.BlockSpec((None, seq, hp * hd), lambda b, h, i: (b, 0, h))
    return pl.pallas_call(
        functools.partial(_diff_kernel, tq=tq, out_scale=1.0 - lambda_init),
        grid=(bsz, width // (hp * hd), seq // tq),
        in_specs=[pl.BlockSpec(memory_space=pltpu.SMEM), qspec, kvspec, kvspec, qspec,
                  pl.BlockSpec((hp, tq, tq), lambda b, h, i: (h, 0, 0), pipeline_mode=pl.Buffered(1)),
                  pl.BlockSpec((hp, ur, tq), lambda b, h, i: (h, 0, 0), pipeline_mode=pl.Buffered(1)),
                  pl.BlockSpec((1, hd), lambda b, h, i: (0, 0))],
        out_specs=qspec,
        out_shape=jax.ShapeDtypeStruct((bsz, seq, width), BF16),
        scratch_shapes=[pltpu.VMEM((n_units, 2 * ur, hd), BF16), pltpu.VMEM((n_units, 2 * ur, LANES), F32),
                        pltpu.VMEM((n_units, 2 * ur, 2 * LANES), F32),
                        pltpu.VMEM((2, n_units, 2 * ur, tq), F32)],
        compiler_params=_cparams(("parallel", "parallel", "arbitrary")),
        name="diff_attn",
    )(lam, q3, k3, v3, sz3, bias_diag, bias_adj, subln_w)


def _merge_kernel(x_ref, y_ref, u_ref, szs_ref, yd_ref, mq_ref, szm_ref, g_ref, mkv_ref,
                  dsk_ref, wglu_ref, wbs_ref, wbd_ref, wbm_ref, wout_ref, lng_ref, lnb_ref, o_ref,
                  *, width, d_model, alpha):
    nt = (((1,), (1,)), ((), ()))
    tm = x_ref.shape[0]
    groups = [pl.ds(i * (tm // MERGE_GROUPS), tm // MERGE_GROUPS) for i in range(MERGE_GROUPS)]
    mm = lambda a, w_ref: jnp.dot(a, w_ref[...], preferred_element_type=F32)
    s_mem = [[lax.dot_general(mq_ref[rows, pl.ds(h * MEM_HEAD_DIM, MEM_HEAD_DIM)],
                              mkv_ref[:, pl.ds(h * MEM_HEAD_DIM, MEM_HEAD_DIM)], nt,
                              preferred_element_type=F32) for h in range(MEM_HEADS)] for rows in groups]
    proj_d = [mm(yd_ref[rows, :], wbd_ref) for rows in groups]
    glu = [mm(jax.nn.gelu(y_ref[rows, :] + dsk_ref[...] * u_ref[rows, :]).astype(BF16), wglu_ref)
           for rows in groups]
    y_mem = []
    for rows, s_heads in zip(groups, s_mem):
        heads = []
        for h, s in enumerate(s_heads):
            p = jnp.exp(s - jnp.max(s, axis=-1, keepdims=True))
            l = jnp.sum(p, axis=-1, keepdims=True)
            o = jnp.dot(p.astype(BF16), mkv_ref[:, pl.ds(width + h * MEM_HEAD_DIM, MEM_HEAD_DIM)],
                        preferred_element_type=F32)
            heads.append(o / l)
        y_mem.append((jnp.concatenate(heads, axis=-1) * szm_ref[rows, :].astype(F32)).astype(BF16))
    y_ssm = [(a[:, :width] * jax.nn.sigmoid(a[:, width:]) * szs_ref[rows, :].astype(F32)).astype(BF16)
             for rows, a in zip(groups, glu)]
    proj_s = [mm(a, wbs_ref) for a in y_ssm]
    proj_m = [mm(a, wbm_ref) for a in y_mem]
    merged = [(g_ref[rows, 0:d_model].astype(F32) * ps
               + g_ref[rows, d_model:2 * d_model].astype(F32) * pd
               + g_ref[rows, 2 * d_model:3 * d_model].astype(F32) * pm).astype(BF16)
              for rows, ps, pd, pm in zip(groups, proj_s, proj_d, proj_m)]
    outs = [mm(a, wout_ref) for a in merged]
    for rows, out in zip(groups, outs):
        hres = alpha * x_ref[rows, :] + out
        mu = jnp.mean(hres, axis=-1, keepdims=True)
        cen = hres - mu
        var = jnp.mean(cen * cen, axis=-1, keepdims=True)
        o_ref[rows, :] = cen * lax.rsqrt(var + LN_EPS) * lng_ref[...] + lnb_ref[...]


def _merge(x2, y2, u2, szs, yd, mq, szm, g, mkv, d_skip, w_glu, w_bs, w_bd, w_bm, w_out, ln_g, ln_b,
           seq, alpha):
    t, d_model = x2.shape
    width = y2.shape[1]
    n_mem = mkv.shape[1]
    tm = MERGE_TM
    per_b = seq // tm
    row = lambda n: pl.BlockSpec((tm, n), lambda i: (i, 0))
    full = lambda a: pl.BlockSpec(a.shape, lambda i: (0,) * a.ndim)
    return pl.pallas_call(
        functools.partial(_merge_kernel, width=width, d_model=d_model, alpha=alpha),
        grid=(t // tm,),
        in_specs=[row(d_model), row(width), row(width), row(width), row(width), row(width), row(width),
                  row(3 * d_model),
                  pl.BlockSpec((None, n_mem, 2 * width), lambda i: (i // per_b, 0, 0)),
                  full(d_skip), full(w_glu), full(w_bs), full(w_bd), full(w_bm), full(w_out),
                  full(ln_g), full(ln_b)],
        out_specs=row(d_model),
        out_shape=jax.ShapeDtypeStruct((t, d_model), F32),
        compiler_params=_cparams(("parallel",)),
        name="merge_out",
    )(x2, y2, u2, szs, yd, mq, szm, g, mkv, d_skip, w_glu, w_bs, w_bd, w_bm, w_out, ln_g, ln_b)


def kernel(x, mem, w_in, lam_re, lam_im, log_dt, b_re, b_im, c_re, c_im, d_skip, w_glu, lambda_q1, lambda_k1, lambda_q2, lambda_k2, subln_w, rel_bias, w_mem_kv, w_br_ssm, w_br_diff, w_br_mem, w_out, ln_g, ln_b):
    bsz, seq, d_model = x.shape
    depth = w_in.shape[0]
    width = d_model // 2
    n_mem = mem.shape[1]
    alpha = (2.0 * depth) ** 0.25
    bias_diag, bias_adj = _bias_tiles(rel_bias, seq, DIFF_TQ)
    mem2 = mem.reshape(bsz * n_mem, d_model)
    h = x.reshape(bsz * seq, d_model)
    for layer in range(depth):
        lambda_init = 0.8 - 0.6 * math.exp(-0.3 * layer)
        u, szs, dq, dk, dv, szd, mq, szm, g = _projection(h, w_in[layer].astype(BF16), width)
        mkv = _memkv(mem2, w_mem_kv[layer].astype(BF16)).reshape(bsz, n_mem, 2 * width)
        mats = _s5_matrices(lam_re[layer], lam_im[layer], log_dt[layer], b_re[layer], b_im[layer],
                            c_re[layer], c_im[layer])
        y = _s5_core(u.reshape(bsz, seq, width), *mats).reshape(bsz * seq, width)
        lam = (jnp.exp(jnp.sum(lambda_q1[layer].astype(F32) * lambda_k1[layer].astype(F32)))
               - jnp.exp(jnp.sum(lambda_q2[layer].astype(F32) * lambda_k2[layer].astype(F32)))
               + lambda_init).reshape(1)
        r3 = lambda a: a.reshape(bsz, seq, width)
        yd = _diff_attention(lam, r3(dq), r3(dk), r3(dv), r3(szd), bias_diag, bias_adj,
                             subln_w[layer].reshape(1, -1).astype(F32), lambda_init)
        h = _merge(h, y, u, szs, yd.reshape(bsz * seq, width), mq, szm, g, mkv,
                   d_skip[layer].reshape(1, -1), w_glu[layer].astype(BF16), w_br_ssm[layer].astype(BF16),
                   w_br_diff[layer].astype(BF16), w_br_mem[layer].astype(BF16), w_out[layer].astype(BF16),
                   ln_g[layer].reshape(1, -1), ln_b[layer].reshape(1, -1), seq, alpha)
    return h.reshape(bsz, seq, d_model)
```

```python
import functools
import math

import jax
import jax.numpy as jnp
from jax import lax
from jax.experimental import pallas as pl
from jax.experimental.pallas import tpu as pltpu

F32 = jnp.float32
BF16 = jnp.bfloat16

LANES = 128
VMEM_LIMIT_BYTES = 56 * 1024 * 1024

CHUNK = 64
SSM_L = 16
DIFF_HEAD_DIM = 64
MEM_HEADS = 4
MEM_HEAD_DIM = 128
REL_BUCKETS = 32
REL_MAX_DIST = 128
LN_EPS = 1e-5
RMS_EPS = 1e-5
NEG_INF = -1e30
LOG2E = math.log2(math.e)

PROJ_TM = 512
MEMKV_TM = 512
DIFF_TQ = 512
DIFF_UNIT = 128
DIFF_HEADS_PER_STEP = 2
MERGE_TM = 512
MERGE_GROUPS = 2


def _cparams(sem):
    return pltpu.CompilerParams(dimension_semantics=sem, vmem_limit_bytes=VMEM_LIMIT_BYTES)


def _silu(z):
    return z * jax.nn.sigmoid(z)


def _proj_kernel(x_ref, w_ref, u_ref, szs_ref, q_ref, k_ref, v_ref, szd_ref, mq_ref, szm_ref, g_ref,
                 *, width, mem_scale, diff_scale):
    xb = x_ref[...].astype(BF16)

    def mm(col, ncols):
        return jnp.dot(xb, w_ref[:, col:col + ncols], preferred_element_type=F32)

    w = width
    a = mm(0, 2 * w)
    u_ref[...] = a[:, :w]
    szs_ref[...] = _silu(a[:, w:]).astype(BF16)
    a = mm(4 * w, 2 * w)
    v_ref[...] = a[:, :w].astype(BF16)
    szd_ref[...] = _silu(a[:, w:]).astype(BF16)
    a = mm(6 * w, 2 * w)
    mq_ref[...] = (a[:, :w] * mem_scale).astype(BF16)
    szm_ref[...] = _silu(a[:, w:]).astype(BF16)
    for j in range(3):
        g_ref[:, 2 * j * w:2 * (j + 1) * w] = jax.nn.sigmoid(mm(8 * w + 2 * j * w, 2 * w)).astype(BF16)
    a = mm(2 * w, 2 * w)
    q_ref[...] = (a[:, :w] * diff_scale).astype(BF16)
    k_ref[...] = a[:, w:].astype(BF16)


def _projection(x2, w_bf, width):
    t, d = x2.shape
    d_in = w_bf.shape[1]
    tm = PROJ_TM
    assert t % tm == 0 and d_in == 14 * width and width % LANES == 0
    row = lambda n: pl.BlockSpec((tm, n), lambda i: (i, 0))
    outs = [jax.ShapeDtypeStruct((t, width), F32)] + [jax.ShapeDtypeStruct((t, width), BF16)] * 7 \
        + [jax.ShapeDtypeStruct((t, 6 * width), BF16)]
    return pl.pallas_call(
        functools.partial(_proj_kernel, width=width, mem_scale=MEM_HEAD_DIM ** -0.5,
                          diff_scale=DIFF_HEAD_DIM ** -0.5 * LOG2E),
        grid=(t // tm,),
        in_specs=[row(d), pl.BlockSpec((d, d_in), lambda i: (0, 0), pipeline_mode=pl.Buffered(1))],
        out_specs=[row(width)] * 8 + [row(6 * width)],
        out_shape=outs,
        compiler_params=_cparams(("parallel",)),
        name="in_proj",
    )(x2, w_bf)


def _memkv_kernel(m_ref, w_ref, o_ref):
    o_ref[...] = jnp.dot(m_ref[...].astype(BF16), w_ref[...], preferred_element_type=F32).astype(BF16)


def _memkv(mem2, w_bf):
    r, d = mem2.shape
    n = w_bf.shape[1]
    tm = MEMKV_TM
    assert r % tm == 0
    return pl.pallas_call(
        _memkv_kernel,
        grid=(r // tm,),
        in_specs=[pl.BlockSpec((tm, d), lambda i: (i, 0)), pl.BlockSpec((d, n), lambda i: (0, 0))],
        out_specs=pl.BlockSpec((tm, n), lambda i: (i, 0)),
        out_shape=jax.ShapeDtypeStruct((r, n), BF16),
        compiler_params=_cparams(("parallel",)),
        name="mem_kv",
    )(mem2, w_bf)


def _s5_factors(lam_re, lam_im, log_dt, b_re, b_im, c_re, c_im):
    L = SSM_L
    g_n, p_n = lam_re.shape
    h_n = b_re.shape[-1]
    dt = jnp.exp(log_dt.astype(F32))

    def powers(lr, li, d, ks):
        mag = jnp.exp(lr * d * ks)
        return mag * jnp.cos(li * d * ks), mag * jnp.sin(li * d * ks)

    def zoh(lr, li, d):
        pr, pi = powers(lr, li, d, 1.0)
        nr, den = pr - 1.0, lr * lr + li * li
        return (nr * lr + pi * li) / den, (pi * lr - nr * li) / den

    def cmul(ar, ai, br, bi):
        return ar * br - ai * bi, ar * bi + ai * br

    la_r = lam_re.astype(F32).reshape(1, g_n * p_n)
    la_i = lam_im.astype(F32).reshape(1, g_n * p_n)
    dta = jnp.repeat(dt, p_n).reshape(1, g_n * p_n)
    ba_r = b_re.astype(F32).transpose(2, 0, 1).reshape(h_n, g_n * p_n)
    ba_i = b_im.astype(F32).transpose(2, 0, 1).reshape(h_n, g_n * p_n)
    bb_r, bb_i = cmul(*zoh(la_r, la_i, dta), ba_r, ba_i)
    pw_r, pw_i = powers(la_r, la_i, dta, (L - 1 - jnp.arange(L, dtype=F32))[:, None])
    w_r, w_i = cmul(pw_r[:, None, :], pw_i[:, None, :], bb_r[None], bb_i[None])
    w_r = w_r.reshape(L * h_n, g_n * p_n)
    w_i = w_i.reshape(L * h_n, g_n * p_n)
    a_re, a_im = powers(la_r, la_i, dta, float(L))

    lb_r = jnp.repeat(lam_re.astype(F32).T, h_n, axis=1)
    lb_i = jnp.repeat(lam_im.astype(F32).T, h_n, axis=1)
    dtb = jnp.repeat(dt, h_n).reshape(1, g_n * h_n)
    cb_r = c_re.astype(F32).transpose(2, 0, 1).reshape(p_n, g_n * h_n)
    cb_i = c_im.astype(F32).transpose(2, 0, 1).reshape(p_n, g_n * h_n)
    bx_r = jnp.repeat(b_re.astype(F32).transpose(2, 1, 0), h_n, axis=-1)
    bx_i = jnp.repeat(b_im.astype(F32).transpose(2, 1, 0), h_n, axis=-1)
    bbx_r, bbx_i = cmul(*zoh(lb_r, lb_i, dtb), bx_r, bx_i)
    pb_r, pb_i = powers(lb_r, lb_i, dtb, jnp.arange(L + 1, dtype=F32)[:, None, None])
    cp_r, cp_i = cmul(cb_r[None], cb_i[None], pb_r, pb_i)
    kt = jnp.sum(cp_r[:L, None] * bbx_r[None] - cp_i[:L, None] * bbx_i[None], axis=2)
    return kt, w_r, w_i, cp_r[1:], -cp_i[1:], a_re, a_im


def _s5_expand(kt_ref, wr_ref, wi_ref, cr_ref, ci_ref, k_ref, b_ref, c_ref, *, n_state):
    L, h_n, _ = kt_ref.shape
    p_n = cr_ref.shape[1]
    gl = LANES // h_n
    assert h_n & (h_n - 1) == 0 and p_n & (p_n - 1) == 0 and n_state & (n_state - 1) == 0
    lane_o = lax.broadcasted_iota(jnp.int32, (1, L * LANES), 1)
    lane_s = lax.broadcasted_iota(jnp.int32, (1, 2 * n_state), 1)
    group_o = jnp.right_shift(jnp.bitwise_and(lane_o, LANES - 1), h_n.bit_length() - 1)
    group_s = jnp.right_shift(jnp.bitwise_and(lane_s, n_state - 1), p_n.bit_length() - 1)
    base = jnp.concatenate([kt_ref[k] for k in range(L)], axis=1)
    cf = (jnp.concatenate([cr_ref[i] for i in range(L)], axis=1),
          jnp.concatenate([ci_ref[i] for i in range(L)], axis=1))
    for g in range(gl):
        base_g = jnp.where(group_o == g, base, 0.0)
        for j in range(L):
            rows = pl.ds((j * gl + g) * h_n, h_n)
            if j > 0:
                k_ref[rows, 0:j * LANES] = jnp.zeros((h_n, j * LANES), BF16)
            k_ref[rows, j * LANES:L * LANES] = base_g[:, 0:(L - j) * LANES].astype(BF16)
            w = jnp.concatenate([wr_ref[j * h_n:(j + 1) * h_n, :], wi_ref[j * h_n:(j + 1) * h_n, :]], axis=1)
            b_ref[rows, :] = jnp.where(group_s == g, w, 0.0).astype(BF16)
        for r in range(2):
            c_ref[pl.ds((r * gl + g) * p_n, p_n), :] = jnp.where(group_o == g, cf[r], 0.0).astype(BF16)


def _s5_kernel(u_ref, kt_ref, wr_ref, wi_ref, cr_ref, ci_ref, ar_ref, ai_ref, y_ref,
               u2_ref, v_ref, sp_ref, y2_ref, k_ref, b_ref, c_ref, *, n_chunks, n_state):
    L = SSM_L

    @pl.when(pl.program_id(1) == 0)
    def _():
        _s5_expand(kt_ref, wr_ref, wi_ref, cr_ref, ci_ref, k_ref, b_ref, c_ref, n_state=n_state)

    for i in range(L):
        u2_ref[:, i * LANES:(i + 1) * LANES] = u_ref[pl.ds(i, n_chunks, stride=L), :].astype(BF16)
    v_ref[...] = jnp.dot(u2_ref[...], b_ref[...], preferred_element_type=F32)
    ar = ar_ref[...]
    ai = ai_ref[...]

    wb = 2 * LANES
    for blk in range(L * LANES // wb):
        cols = slice(blk * wb, (blk + 1) * wb)
        rows = (blk + 1) * wb
        y2_ref[:, cols] = jnp.dot(u2_ref[:, 0:rows], k_ref[0:rows, cols], preferred_element_type=F32)
    sr = si = jnp.zeros((1, n_state), F32)
    for c in range(n_chunks):
        sp_ref[c:c + 1, 0:n_state] = sr
        sp_ref[c:c + 1, n_state:2 * n_state] = si
        vr = v_ref[c:c + 1, 0:n_state]
        vi = v_ref[c:c + 1, n_state:2 * n_state]
        sr, si = ar * sr - ai * si + vr, ar * si + ai * sr + vi
    y2_ref[...] += jnp.dot(sp_ref[...].astype(BF16), c_ref[...], preferred_element_type=F32)
    for i in range(L):
        y_ref[pl.ds(i, n_chunks, stride=L), :] = y2_ref[:, i * LANES:(i + 1) * LANES]


def _s5_core(u3, kt, w_r, w_i, c_r, c_i, a_re, a_im):
    bsz, seq, width = u3.shape
    nb = width // LANES
    n_chunks = seq // SSM_L
    n_state = a_re.shape[-1] // nb
    kdim = SSM_L * LANES
    assert seq % SSM_L == 0 and n_chunks % 16 == 0 and width % LANES == 0
    lane_block = lambda a: pl.BlockSpec(a.shape[:-1] + (a.shape[-1] // nb,),
                                        lambda m, b: (0,) * (a.ndim - 1) + (m,))
    return pl.pallas_call(
        functools.partial(_s5_kernel, n_chunks=n_chunks, n_state=n_state),
        grid=(nb, bsz),
        in_specs=[pl.BlockSpec((None, seq, LANES), lambda m, b: (b, 0, m)),
                  lane_block(kt), lane_block(w_r), lane_block(w_i), lane_block(c_r), lane_block(c_i),
                  lane_block(a_re), lane_block(a_im)],
        out_specs=pl.BlockSpec((None, seq, LANES), lambda m, b: (b, 0, m)),
        out_shape=jax.ShapeDtypeStruct((bsz, seq, width), F32),
        scratch_shapes=[pltpu.VMEM((n_chunks, kdim), BF16), pltpu.VMEM((n_chunks, 2 * n_state), F32),
                        pltpu.VMEM((n_chunks, 2 * n_state), F32), pltpu.VMEM((n_chunks, kdim), F32),
                        pltpu.VMEM((kdim, kdim), BF16), pltpu.VMEM((kdim, 2 * n_state), BF16),
                        pltpu.VMEM((2 * n_state, kdim), BF16)],
        compiler_params=_cparams(("parallel", "arbitrary")),
        name="s5_core",
    )(u3, kt, w_r, w_i, c_r, c_i, a_re, a_im)


def _t5_bucket(rel):
    half = REL_BUCKETS // 2
    max_exact = half // 2
    ret = jnp.where(rel > 0, half, 0)
    n = jnp.abs(rel)
    large = max_exact + (jnp.log(jnp.maximum(n, 1).astype(jnp.float32) / max_exact)
                         / math.log(REL_MAX_DIST / max_exact) * (half - max_exact)).astype(jnp.int32)
    large = jnp.minimum(large, half - 1)
    return ret + jnp.where(n < max_exact, n, large)


def _bias_tiles(rel_bias, seq, tq):
    assert tq >= REL_MAX_DIST and tq % CHUNK == 0 and DIFF_UNIT <= tq
    n = 2 * tq
    heads = rel_bias.shape[1]
    k = jnp.arange(n)
    rel_d = jnp.where(k < tq, k, k - n)
    rel = jnp.concatenate([rel_d, rel_d - tq, jnp.full((1,), -(seq - 1), jnp.int32)])
    onehot = (_t5_bucket(rel)[:, None] == jnp.arange(REL_BUCKETS)[None, :]).astype(F32)
    table = jnp.dot(onehot, rel_bias.astype(F32), precision=lax.Precision.HIGHEST)
    table = ((table[:2 * n] - table[2 * n:]) * LOG2E).T.reshape(heads, 2, 1, n)

    def expand(w_ref, d_ref, a_ref):
        rolled = pltpu.roll(jnp.broadcast_to(w_ref[0], (tq, n)), 0, 1, stride=1, stride_axis=0)
        r = lax.broadcasted_iota(jnp.int32, (tq, tq), 0)
        c = lax.broadcasted_iota(jnp.int32, (tq, tq), 1)
        allowed = jnp.right_shift(c, CHUNK.bit_length() - 1) <= jnp.right_shift(r, CHUNK.bit_length() - 1)
        d_ref[...] = jnp.where(allowed, rolled[:, 0:tq], NEG_INF)
        a_ref[...] = pltpu.roll(jnp.broadcast_to(w_ref[1], (DIFF_UNIT, n)), 0, 1,
                                stride=1, stride_axis=0)[:, 0:tq]

    assert CHUNK & (CHUNK - 1) == 0
    return pl.pallas_call(
        expand,
        grid=(heads,),
        in_specs=[pl.BlockSpec((None, 2, 1, n), lambda h: (h, 0, 0, 0))],
        out_specs=[pl.BlockSpec((None, tq, tq), lambda h: (h, 0, 0)),
                   pl.BlockSpec((None, DIFF_UNIT, tq), lambda h: (h, 0, 0))],
        out_shape=[jax.ShapeDtypeStruct((heads, tq, tq), F32),
                   jax.ShapeDtypeStruct((heads, DIFF_UNIT, tq), F32)],
        compiler_params=_cparams(("parallel",)),
        name="bias_tiles",
    )(table)


def _diff_kernel(lam_ref, q_ref, k_ref, v_ref, z_ref, bd_ref, ba_ref, w_ref, o_ref,
                 qs_ref, m_ref, acc_ref, s_ref, *, tq, out_scale):
    qi = pl.program_id(2)
    lam = lam_ref[0]
    ur = DIFF_UNIT
    hd = 2 * DIFF_HEAD_DIM
    n_heads = q_ref.shape[1] // hd
    n_units = (tq // ur) * n_heads
    unit_row = lambda u: u // n_heads
    unit_cols = lambda u: pl.ds((u % n_heads) * hd, hd)
    nt = (((1,), (1,)), ((), ()))
    lane = lax.broadcasted_iota(jnp.int32, (ur, LANES), 1)

    for u in range(n_units):
        qh = q_ref[pl.ds(unit_row(u) * ur, ur), unit_cols(u)].astype(F32)
        qs_ref[u, 0:ur, :] = jnp.where(lane < DIFF_HEAD_DIM, qh, 0.0).astype(BF16)
        qs_ref[u, ur:2 * ur, :] = jnp.where(lane >= DIFF_HEAD_DIM, qh, 0.0).astype(BF16)
    m_ref[...] = jnp.full(m_ref.shape, NEG_INF, F32)
    acc_ref[...] = jnp.zeros(acc_ref.shape, F32)

    def scores(u, k0, nk):
        return lax.dot_general(qs_ref[u], k_ref[pl.ds(k0, nk), unit_cols(u)], nt,
                               preferred_element_type=F32)

    def update(u, s, k0, nk):
        blocks = [s[:, j * LANES:(j + 1) * LANES] for j in range(nk // LANES)]
        m_old = m_ref[u]
        m_new = jnp.maximum(m_old, jnp.max(functools.reduce(jnp.maximum, blocks), axis=-1, keepdims=True))
        alpha = jnp.exp2(m_old - m_new)
        p = jnp.concatenate([jnp.exp2(blk - m_new) for blk in blocks], axis=1).astype(BF16)
        v_ext = jnp.concatenate([v_ref[pl.ds(k0, nk), unit_cols(u)], jnp.ones((nk, LANES), BF16)], axis=1)
        pv = jnp.dot(p, v_ext, preferred_element_type=F32)
        acc_ref[u] = jnp.concatenate([alpha, alpha], axis=1) * acc_ref[u] + pv
        m_ref[u] = m_new

    def produce(u, j, slot):
        k0 = pl.multiple_of(j * tq, tq)
        s = scores(u, k0, tq)
        if unit_row(u) == 0:
            bias = ba_ref[u % n_heads] * (j == qi - 1).astype(F32)
            s = s + jnp.concatenate([bias, bias], axis=0)
        s_ref[slot, u] = s

    def consume(u, j, slot):
        update(u, s_ref[slot, u], pl.multiple_of(j * tq, tq), tq)

    def step(j, slot):
        for u in range(n_units):
            consume(u, j, slot)
            produce(u, j + 1, 1 - slot)

    kd = pl.multiple_of(qi * tq, tq)
    s_diag = []
    for u in range(n_units):
        r = unit_row(u)
        nk = (r + 1) * ur
        bias = bd_ref[u % n_heads, r * ur:(r + 1) * ur, 0:nk]
        s_diag.append(scores(u, kd, nk) + jnp.concatenate([bias, bias], axis=0))
    for u in range(n_units):
        produce(u, 0, 0)
    for u in range(n_units):
        update(u, s_diag[u], kd, (unit_row(u) + 1) * ur)

    def pair(i, carry):
        step(2 * i, 0)
        step(2 * i + 1, 1)
        return carry

    lax.fori_loop(0, (qi - 1) // 2, pair, 0)

    def finalize(u):
        acc = acc_ref[u]
        o = acc[0:ur, 0:LANES] / acc[0:ur, LANES:] - lam * (acc[ur:, 0:LANES] / acc[ur:, LANES:])
        o = o * lax.rsqrt(jnp.mean(o * o, axis=-1, keepdims=True) + RMS_EPS) * w_ref[...]
        o = o * out_scale
        rows = pl.ds(unit_row(u) * ur, ur)
        o_ref[rows, unit_cols(u)] = (o * z_ref[rows, unit_cols(u)].astype(F32)).astype(o_ref.dtype)

    def drain(j, slot):
        for u in range(n_units):
            consume(u, j, slot)
            finalize(u)

    @pl.when(qi == 0)
    def _():
        for u in range(n_units):
            finalize(u)

    @pl.when(qi % 2 == 1)
    def _():
        drain(qi - 1, 0)

    @pl.when(jnp.logical_and(qi >= 2, qi % 2 == 0))
    def _():
        step(qi - 2, 0)
        drain(qi - 1, 1)


def _diff_attention(lam, q3, k3, v3, sz3, bias_diag, bias_adj, subln_w, lambda_init):
    bsz, seq, width = q3.shape
    tq = DIFF_TQ
    ur = DIFF_UNIT
    hd = 2 * DIFF_HEAD_DIM
    assert hd == LANES and ur >= REL_MAX_DIST
    hp = DIFF_HEADS_PER_STEP
    assert seq % tq == 0 and tq % ur == 0 and width % (hp * hd) == 0
    n_units = (tq // ur) * hp
    qspec = pl.BlockSpec((None, tq, hp * hd), lambda b, h, i: (b, i, h))
    kvspec = pl.BlockSpec((None, seq, hp * hd), lambda b, h, i: (b, 0, h))
    return pl.pallas_call(
        functools.partial(_diff_kernel, tq=tq, out_scale=1.0 - lambda_init),
        grid=(bsz, width // (hp * hd), seq // tq),
        in_specs=[pl.BlockSpec(memory_space=pltpu.SMEM), qspec, kvspec, kvspec, qspec,
                  pl.BlockSpec((hp, tq, tq), lambda b, h, i: (h, 0, 0), pipeline_mode=pl.Buffered(1)),
                  pl.BlockSpec((hp, ur, tq), lambda b, h, i: (h, 0, 0), pipeline_mode=pl.Buffered(1)),
                  pl.BlockSpec((1, hd), lambda b, h, i: (0, 0))],
        out_specs=qspec,
        out_shape=jax.ShapeDtypeStruct((bsz, seq, width), BF16),
        scratch_shapes=[pltpu.VMEM((n_units, 2 * ur, hd), BF16), pltpu.VMEM((n_units, 2 * ur, LANES), F32),
                        pltpu.VMEM((n_units, 2 * ur, 2 * LANES), F32),
                        pltpu.VMEM((2, n_units, 2 * ur, tq), F32)],
        compiler_params=_cparams(("parallel", "parallel", "arbitrary")),
        name="diff_attn",
    )(lam, q3, k3, v3, sz3, bias_diag, bias_adj, subln_w)


def _merge_kernel(x_ref, y_ref, u_ref, szs_ref, yd_ref, mq_ref, szm_ref, g_ref, mkv_ref,
                  dsk_ref, wglu_ref, wbs_ref, wbd_ref, wbm_ref, wout_ref, lng_ref, lnb_ref, o_ref,
                  *, width, d_model, alpha):
    nt = (((1,), (1,)), ((), ()))
    tm = x_ref.shape[0]
    groups = [pl.ds(i * (tm // MERGE_GROUPS), tm // MERGE_GROUPS) for i in range(MERGE_GROUPS)]
    mm = lambda a, w_ref: jnp.dot(a, w_ref[...], preferred_element_type=F32)
    s_mem = [[lax.dot_general(mq_ref[rows, pl.ds(h * MEM_HEAD_DIM, MEM_HEAD_DIM)],
                              mkv_ref[:, pl.ds(h * MEM_HEAD_DIM, MEM_HEAD_DIM)], nt,
                              preferred_element_type=F32) for h in range(MEM_HEADS)] for rows in groups]
    proj_d = [mm(yd_ref[rows, :], wbd_ref) for rows in groups]
    glu = [mm(jax.nn.gelu(y_ref[rows, :] + dsk_ref[...] * u_ref[rows, :]).astype(BF16), wglu_ref)
           for rows in groups]
    y_mem = []
    for rows, s_heads in zip(groups, s_mem):
        heads = []
        for h, s in enumerate(s_heads):
            p = jnp.exp(s - jnp.max(s, axis=-1, keepdims=True))
            l = jnp.sum(p, axis=-1, keepdims=True)
            o = jnp.dot(p.astype(BF16), mkv_ref[:, pl.ds(width + h * MEM_HEAD_DIM, MEM_HEAD_DIM)],
                        preferred_element_type=F32)
            heads.append(o / l)
        y_mem.append((jnp.concatenate(heads, axis=-1) * szm_ref[rows, :].astype(F32)).astype(BF16))
    y_ssm = [(a[:, :width] * jax.nn.sigmoid(a[:, width:]) * szs_ref[rows, :].astype(F32)).astype(BF16)
             for rows, a in zip(groups, glu)]
    proj_s = [mm(a, wbs_ref) for a in y_ssm]
    proj_m = [mm(a, wbm_ref) for a in y_mem]
    merged = [(g_ref[rows, 0:d_model].astype(F32) * ps
               + g_ref[rows, d_model:2 * d_model].astype(F32) * pd
               + g_ref[rows, 2 * d_model:3 * d_model].astype(F32) * pm).astype(BF16)
              for rows, ps, pd, pm in zip(groups, proj_s, proj_d, proj_m)]
    outs = [mm(a, wout_ref) for a in merged]
    for rows, out in zip(groups, outs):
        hres = alpha * x_ref[rows, :] + out
        mu = jnp.mean(hres, axis=-1, keepdims=True)
        cen = hres - mu
        var = jnp.mean(cen * cen, axis=-1, keepdims=True)
        o_ref[rows, :] = cen * lax.rsqrt(var + LN_EPS) * lng_ref[...] + lnb_ref[...]


def _merge(x2, y2, u2, szs, yd, mq, szm, g, mkv, d_skip, w_glu, w_bs, w_bd, w_bm, w_out, ln_g, ln_b,
           seq, alpha):
    t, d_model = x2.shape
    width = y2.shape[1]
    n_mem = mkv.shape[1]
    tm = MERGE_TM
    per_b = seq // tm
    assert seq % tm == 0 and tm % MERGE_GROUPS == 0 and width == MEM_HEADS * MEM_HEAD_DIM and d_model == 2 * width
    row = lambda n: pl.BlockSpec((tm, n), lambda i: (i, 0))
    full = lambda a: pl.BlockSpec(a.shape, lambda i: (0,) * a.ndim)
    return pl.pallas_call(
        functools.partial(_merge_kernel, width=width, d_model=d_model, alpha=alpha),
        grid=(t // tm,),
        in_specs=[row(d_model), row(width), row(width), row(width), row(width), row(width), row(width),
                  row(3 * d_model),
                  pl.BlockSpec((None, n_mem, 2 * width), lambda i: (i // per_b, 0, 0)),
                  full(d_skip), full(w_glu), full(w_bs), full(w_bd), full(w_bm), full(w_out),
                  full(ln_g), full(ln_b)],
        out_specs=row(d_model),
        out_shape=jax.ShapeDtypeStruct((t, d_model), F32),
        compiler_params=_cparams(("parallel",)),
        name="merge_out",
    )(x2, y2, u2, szs, yd, mq, szm, g, mkv, d_skip, w_glu, w_bs, w_bd, w_bm, w_out, ln_g, ln_b)


def kernel(x, mem, w_in, lam_re, lam_im, log_dt, b_re, b_im, c_re, c_im, d_skip, w_glu, lambda_q1, lambda_k1, lambda_q2, lambda_k2, subln_w, rel_bias, w_mem_kv, w_br_ssm, w_br_diff, w_br_mem, w_out, ln_g, ln_b):
    bsz, seq, d_model = x.shape
    depth = w_in.shape[0]
    width = d_model // 2
    n_mem = mem.shape[1]
    alpha = (2.0 * depth) ** 0.25
    bias_diag, bias_adj = _bias_tiles(rel_bias, seq, DIFF_TQ)
    mem2 = mem.reshape(bsz * n_mem, d_model)
    h = x.reshape(bsz * seq, d_model)
    for layer in range(depth):
        lambda_init = 0.8 - 0.6 * math.exp(-0.3 * layer)
        u, szs, dq, dk, dv, szd, mq, szm, g = _projection(h, w_in[layer].astype(BF16), width)
        mkv = _memkv(mem2, w_mem_kv[layer].astype(BF16)).reshape(bsz, n_mem, 2 * width)
        factors = _s5_factors(lam_re[layer], lam_im[layer], log_dt[layer], b_re[layer], b_im[layer],
                              c_re[layer], c_im[layer])
        y = _s5_core(u.reshape(bsz, seq, width), *factors).reshape(bsz * seq, width)
        lam = (jnp.exp(jnp.sum(lambda_q1[layer].astype(F32) * lambda_k1[layer].astype(F32)))
               - jnp.exp(jnp.sum(lambda_q2[layer].astype(F32) * lambda_k2[layer].astype(F32)))
               + lambda_init).reshape(1)
        r3 = lambda a: a.reshape(bsz, seq, width)
        yd = _diff_attention(lam, r3(dq), r3(dk), r3(dv), r3(szd), bias_diag, bias_adj,
                             subln_w[layer].reshape(1, -1).astype(F32), lambda_init)
        h = _merge(h, y, u, szs, yd.reshape(bsz * seq, width), mq, szm, g, mkv,
                   d_skip[layer].reshape(1, -1), w_glu[layer].astype(BF16), w_br_ssm[layer].astype(BF16),
                   w_br_diff[layer].astype(BF16), w_br_mem[layer].astype(BF16), w_out[layer].astype(BF16),
                   ln_g[layer].reshape(1, -1), ln_b[layer].reshape(1, -1), seq, alpha)
    return h.reshape(bsz, seq, d_model)
```
